```python
import math
import jax, jax.numpy as jnp
from jax import lax
import numpy as np

D_MODEL = 1024
BATCH = 2
SEQ = 8192
DEPTH = 2

CHUNK = 64
MEM_LEN = 256
N_MIXERS = 4
GROUP_W = D_MODEL // N_MIXERS
POOL_GROUPS = 4
POOL_CH = GROUP_W // POOL_GROUPS
POOL_WINDOWS = (2, 4, 8, 16)
SCONV_W = 3
DIFF_HEADS = 4
DIFF_DV = GROUP_W // DIFF_HEADS
DIFF_DK = DIFF_DV // 2
ROT_DIMS = DIFF_DK // 4
ROPE_THETA = 500000.0
CONF_W = 31
X_HEADS = 4
X_HD = D_MODEL // X_HEADS
D_FF = ((8 * D_MODEL // 3 + 127) // 128) * 128
Q_BLOCK = 128
EPS = 1e-6

POOL_COLS = GROUP_W
SCONV_COLS = 3 * GROUP_W
DIFF_COLS = 3 * GROUP_W
CONF_COLS = 2 * GROUP_W
IN_COLS = POOL_COLS + SCONV_COLS + DIFF_COLS + CONF_COLS
SPLITS = [POOL_COLS, POOL_COLS + SCONV_COLS, POOL_COLS + SCONV_COLS + DIFF_COLS]

kernel_name = "hybrid_parallel_group_streaming_encoder"


def rms_norm(x, g):
    x32 = x.astype(jnp.float32)
    y = x32 * lax.rsqrt(jnp.mean(x32 * x32, axis=-1, keepdims=True) + EPS)
    return (y * g.astype(jnp.float32)).astype(x.dtype)


def layer_norm(x, g, b):
    x32 = x.astype(jnp.float32)
    mu = jnp.mean(x32, axis=-1, keepdims=True)
    xc = x32 - mu
    y = xc * lax.rsqrt(jnp.mean(xc * xc, axis=-1, keepdims=True) + EPS)
    return (y * g.astype(jnp.float32) + b.astype(jnp.float32)).astype(x.dtype)


def swiglu(h, w_gate, w_up, w_down):
    return (jax.nn.silu(h @ w_gate) * (h @ w_up)) @ w_down


def causal_depthwise_conv(u, w):
    k = w.shape[0]
    return lax.conv_general_dilated(
        u, w[:, None, :].astype(u.dtype), window_strides=(1,), padding=[(k - 1, 0)],
        dimension_numbers=("NWC", "WIO", "NWC"), feature_group_count=u.shape[-1])


def pool_mixer(u, pool_w, pool_scale):
    b, s, _ = u.shape
    ug = u.reshape(b, s, POOL_GROUPS, POOL_CH)
    u32 = ug.astype(jnp.float32)
    cs = jnp.cumsum(u32, axis=1)
    t = jnp.arange(s)
    means = []
    for g, w in enumerate(POOL_WINDOWS):
        c = cs[:, :, g]
        lo = jnp.pad(c, ((0, 0), (w, 0), (0, 0)))[:, :s]
        cnt = jnp.minimum(t + 1, w).astype(jnp.float32)
        means.append((c - lo) / cnt[None, :, None])
    mixed = (jnp.stack(means, axis=2) - u32).astype(u.dtype)
    y = jnp.einsum("bsgc,gcd->bsgd", mixed, pool_w)
    return y.reshape(b, s, GROUP_W) * pool_scale


def partial_rope(x, cos, sin):
    half = ROT_DIMS // 2
    xr = x[..., :ROT_DIMS].astype(jnp.float32)
    x1, x2 = xr[..., :half], xr[..., half:]
    c = cos[:, :, None, None, :]
    s = sin[:, :, None, None, :]
    rot = jnp.concatenate([x1 * c - x2 * s, x2 * c + x1 * s], axis=-1)
    return jnp.concatenate([rot.astype(x.dtype), x[..., ROT_DIMS:]], axis=-1)


def diff_attention(q, k, v, lam):
    b, s = q.shape[:2]
    nb = s // Q_BLOCK
    qb = jnp.moveaxis(q.reshape(b, nb, Q_BLOCK, DIFF_HEADS, 2, DIFF_DK), 1, 0)
    kpos = jnp.arange(s)
    scale = DIFF_DK ** -0.5

    def block(args):
        qblk, i = args
        sc = jnp.einsum("bqhcd,bkhcd->bchqk", qblk, k).astype(jnp.float32) * scale
        qpos = i * Q_BLOCK + jnp.arange(Q_BLOCK)
        limit = (qpos // CHUNK + 1) * CHUNK
        mask = kpos[None, :] < limit[:, None]
        sc = jnp.where(mask, sc, -jnp.inf)
        p = jax.nn.softmax(sc, axis=-1)
        a = p[:, 0] - lam * p[:, 1]
        return jnp.einsum("bhqk,bkhd->bqhd", a.astype(v.dtype), v)

    o = lax.map(block, (qb, jnp.arange(nb)))
    return jnp.moveaxis(o, 0, 1).reshape(b, s, DIFF_HEADS, DIFF_DV)


def setup_inputs(seed: int = 0) -> dict:
    key = jax.random.key(seed)
    ks = iter(jax.random.split(key, 64))

    def nrm(shape, scale):
        return scale * jax.random.normal(next(ks), shape, jnp.float32)

    def gain(shape):
        return 1.0 + nrm(shape, 0.02)

    L, D, F, G = DEPTH, D_MODEL, D_FF, GROUP_W
    return {
        "x": nrm((BATCH, SEQ, D), 1.0),
        "mem": nrm((BATCH, MEM_LEN, D), 1.0),
        "positions": jnp.broadcast_to(jnp.arange(SEQ, dtype=jnp.int32), (BATCH, SEQ)),
        "f1_pre_g": gain((L, D)),
        "f1_w_gate": nrm((L, D, F), D ** -0.5),
        "f1_w_up": nrm((L, D, F), D ** -0.5),
        "f1_w_down": nrm((L, F, D), F ** -0.5),
        "f1_post_g": gain((L, D)),
        "mix_pre_g": gain((L, D)),
        "w_in": nrm((L, D, IN_COLS), D ** -0.5),
        "pool_w": nrm((L, POOL_GROUPS, POOL_CH, POOL_CH), POOL_CH ** -0.5),
        "pool_scale": gain((L, G)),
        "sconv_w": nrm((L, SCONV_W, G), SCONV_W ** -0.5),
        "lam_q1": nrm((L, DIFF_DK), 0.1),
        "lam_k1": nrm((L, DIFF_DK), 0.1),
        "lam_q2": nrm((L, DIFF_DK), 0.1),
        "lam_k2": nrm((L, DIFF_DK), 0.1),
        "diff_subln_g": gain((L, DIFF_DV)),
        "conf_dw_w": nrm((L, CONF_W, G), CONF_W ** -0.5),
        "conf_dw_b": nrm((L, G), 0.02),
        "conf_ln_g": gain((L, G)),
        "conf_ln_b": nrm((L, G), 0.02),
        "w_o": nrm((L, D, D), D ** -0.5),
        "mix_post_g": gain((L, D)),
        "x_pre_g": gain((L, D)),
        "x_mem_g": gain((L, D)),
        "x_wq": nrm((L, D, D), D ** -0.5),
        "x_wk": nrm((L, D, D), D ** -0.5),
        "x_wv": nrm((L, D, D), D ** -0.5),
        "x_wo": nrm((L, D, D), D ** -0.5),
        "x_post_g": gain((L, D)),
        "f2_pre_g": gain((L, D)),
        "f2_w_gate": nrm((L, D, F), D ** -0.5),
        "f2_w_up": nrm((L, D, F), D ** -0.5),
        "f2_w_down": nrm((L, F, D), F ** -0.5),
        "f2_post_g": gain((L, D)),
    }


def reference(x, mem, positions,
              f1_pre_g, f1_w_gate, f1_w_up, f1_w_down, f1_post_g,
              mix_pre_g, w_in, pool_w, pool_scale, sconv_w,
              lam_q1, lam_k1, lam_q2, lam_k2, diff_subln_g,
              conf_dw_w, conf_dw_b, conf_ln_g, conf_ln_b, w_o, mix_post_g,
              x_pre_g, x_mem_g, x_wq, x_wk, x_wv, x_wo, x_post_g,
              f2_pre_g, f2_w_gate, f2_w_up, f2_w_down, f2_post_g):
    b, s, d = x.shape
    m_len = mem.shape[1]
    inv_freq = ROPE_THETA ** (-jnp.arange(0, ROT_DIMS, 2, dtype=jnp.float32) / ROT_DIMS)
    ang = positions.astype(jnp.float32)[..., None] * inv_freq
    cos, sin = jnp.cos(ang), jnp.sin(ang)

    for l in range(DEPTH):
        h = rms_norm(x, f1_pre_g[l])
        x = x + 0.5 * rms_norm(swiglu(h, f1_w_gate[l], f1_w_up[l], f1_w_down[l]), f1_post_g[l])

        h = rms_norm(x, mix_pre_g[l])
        z = h @ w_in[l]
        za, zb, zc, zd = jnp.split(z, SPLITS, axis=-1)

        ya = pool_mixer(za, pool_w[l], pool_scale[l])

        xs, bg, cg = jnp.split(zb, 3, axis=-1)
        yb = bg * causal_depthwise_conv(cg * xs, sconv_w[l])

        zq, zk, zv = jnp.split(zc, 3, axis=-1)
        q = partial_rope(zq.reshape(b, s, DIFF_HEADS, 2, DIFF_DK), cos, sin)
        k = partial_rope(zk.reshape(b, s, DIFF_HEADS, 2, DIFF_DK), cos, sin)
        v = zv.reshape(b, s, DIFF_HEADS, DIFF_DV)
        lambda_init = 0.8 - 0.6 * math.exp(-0.3 * l)
        lam = (jnp.exp(jnp.sum(lam_q1[l].astype(jnp.float32) * lam_k1[l].astype(jnp.float32)))
               - jnp.exp(jnp.sum(lam_q2[l].astype(jnp.float32) * lam_k2[l].astype(jnp.float32)))
               + lambda_init)
        o = diff_attention(q, k, v, lam)
        yc = (rms_norm(o, diff_subln_g[l]) * (1.0 - lambda_init)).reshape(b, s, GROUP_W)

        ga, gg = jnp.split(zd, 2, axis=-1)
        u = ga * jax.nn.sigmoid(gg)
        u = causal_depthwise_conv(u, conf_dw_w[l]) + conf_dw_b[l]
        yd = jax.nn.silu(layer_norm(u, conf_ln_g[l], conf_ln_b[l]))

        y = jnp.concatenate([ya, yb, yc, yd], axis=-1) @ w_o[l]
        x = x + rms_norm(y, mix_post_g[l])

        h = rms_norm(x, x_pre_g[l])
        mn = rms_norm(mem, x_mem_g[l])
        qx = (h @ x_wq[l]).reshape(b, s, X_HEADS, X_HD)
        kx = (mn @ x_wk[l]).reshape(b, m_len, X_HEADS, X_HD)
        vx = (mn @ x_wv[l]).reshape(b, m_len, X_HEADS, X_HD)
        sc = jnp.einsum("bshd,bmhd->bhsm", qx, kx).astype(jnp.float32) * (X_HD ** -0.5)
        p = jax.nn.softmax(sc, axis=-1)
        ox = jnp.einsum("bhsm,bmhd->bshd", p.astype(vx.dtype), vx).reshape(b, s, d) @ x_wo[l]
        x = x + rms_norm(ox, x_post_g[l])

        h = rms_norm(x, f2_pre_g[l])
        x = x + 0.5 * rms_norm(swiglu(h, f2_w_gate[l], f2_w_up[l], f2_w_down[l]), f2_post_g[l])
    return x
```

```python
import functools
import math

import jax
import jax.numpy as jnp
from jax import lax
from jax.experimental import pallas as pl
from jax.experimental.pallas import tpu as pltpu

F32 = jnp.float32
BF16 = jnp.bfloat16

EPS = 1e-6
CHUNK = 64
N_MIXERS = 4
POOL_WINDOWS = (2, 4, 8, 16)
DIFF_HEADS = 4
ROPE_THETA = 500000.0
X_HEADS = 4

V7X_LANES = 128
V7X_SUBLANES = 8
V7X_MXU_DIM = 256
V7X_VMEM_BYTES = 64 * 1024 * 1024

HALO = 32
NEG_BIG = -1e30
LOG2E = math.log2(math.e)


def _vmem_limit(estimate_bytes):
    return int(min(estimate_bytes + (12 << 20), V7X_VMEM_BYTES - (6 << 20)))


def _rms(x, g):
    return x * lax.rsqrt(jnp.mean(x * x, axis=-1, keepdims=True) + EPS) * g


def _dot(a, b):
    return jnp.dot(a, b, preferred_element_type=F32)


def _row_tile(rows_per_seq):
    for cand in (512, 256, 128):
        if rows_per_seq % cand == 0:
            return cand
    raise ValueError(f"sequence length {rows_per_seq} must be a multiple of 128")


def _ffn_kernel(x_ref, pre_g_ref, wg_ref, wu_ref, wd_ref, post_g_ref, o_ref):
    x = x_ref[...]
    h = _rms(x, pre_g_ref[...]).astype(BF16)
    g = _dot(h, wg_ref[...])
    u = _dot(h, wu_ref[...])
    a = (g * jax.nn.sigmoid(g) * u).astype(BF16)
    y = _dot(a, wd_ref[...])
    o_ref[...] = x + 0.5 * _rms(y, post_g_ref[...])


def _ffn(x, layer, pre_g, wg, wu, wd, post_g, tm):
    n, d = x.shape
    f = wg.shape[-1]
    const = lambda i: (layer, 0, 0)
    est = 3 * d * f * 2 * 2 + 4 * tm * d * 4 + 3 * tm * f * 4
    return pl.pallas_call(
        _ffn_kernel,
        grid=(n // tm,),
        in_specs=[
            pl.BlockSpec((tm, d), lambda i: (i, 0)),
            pl.BlockSpec((None, 1, d), const),
            pl.BlockSpec((None, d, f), const),
            pl.BlockSpec((None, d, f), const),
            pl.BlockSpec((None, f, d), const),
            pl.BlockSpec((None, 1, d), const),
        ],
        out_specs=pl.BlockSpec((tm, d), lambda i: (i, 0)),
        out_shape=jax.ShapeDtypeStruct((n, d), F32),
        compiler_params=pltpu.CompilerParams(
            dimension_semantics=("arbitrary",), vmem_limit_bytes=_vmem_limit(est)),
        name="ffn",
    )(x, pre_g, wg, wu, wd, post_g)


def _mix_in_kernel(x_ref, g_ref, w_in_ref, ct_ref, st_ref, poolw_ref, pscale_ref,
                   sconv_ref, confw_ref, confb_ref, lng_ref, lnb_ref,
                   yabd_ref, qt_ref, k_ref, vt_ref, ext_ref, *, gw, dk, rot):
    tm = x_ref.shape[0]
    t_idx = pl.program_id(1)

    @pl.when(t_idx == 0)
    def _():
        ext_ref[0:HALO, :] = jnp.zeros((HALO, ext_ref.shape[1]), F32)

    x = x_ref[...]
    h = _rms(x, g_ref[...]).astype(BF16)

    def proj(c):
        return _dot(h, w_in_ref[:, c * gw:(c + 1) * gw])

    def hist(col, shift):
        return ext_ref[HALO - shift:HALO - shift + tm, col * gw:(col + 1) * gw]

    za = proj(0)
    ext_ref[HALO:, 0:gw] = za
    pool_ch = gw // len(POOL_WINDOWS)
    frame = lax.broadcasted_iota(jnp.int32, (tm, V7X_LANES), 0) + t_idx * tm
    lane = lax.broadcasted_iota(jnp.int32, (tm, V7X_LANES), 1)
    halves = []
    for hv in range(gw // V7X_LANES):
        groups = [g for g in range(len(POOL_WINDOWS)) if (g * pool_ch) // V7X_LANES == hv]
        wins = [POOL_WINDOWS[g] for g in groups]
        lo = hv * V7X_LANES
        run = ext_ref[HALO:HALO + tm, lo:lo + V7X_LANES]
        cur = run
        prefix = {1: run}
        for j in range(1, max(wins)):
            run = run + ext_ref[HALO - j:HALO - j + tm, lo:lo + V7X_LANES]
            prefix[j + 1] = run
        csum = prefix[wins[-1]]
        win = jnp.full((tm, V7X_LANES), wins[-1], jnp.int32)
        for g, w in zip(groups[:-1][::-1], wins[:-1][::-1]):
            sel = lane < (g + 1) * pool_ch - lo
            csum = jnp.where(sel, prefix[w], csum)
            win = jnp.where(sel, w, win)
        cnt = jnp.minimum(frame + 1, win).astype(F32)
        halves.append(csum / cnt - cur)
    mixed = jnp.concatenate(halves, axis=1).astype(BF16)
    ya = _dot(mixed, poolw_ref[...]) * pscale_ref[...]
    yabd_ref[:, 0:gw] = ya.astype(BF16)

    xs = proj(1)
    bg = proj(2)
    cg = proj(3)
    ext_ref[HALO:, gw:2 * gw] = cg * xs
    n_s = sconv_ref.shape[0]
    conv = sconv_ref[n_s - 1:n_s, :] * hist(1, 0)
    for j in range(n_s - 1):
        conv = conv + sconv_ref[j:j + 1, :] * hist(1, n_s - 1 - j)
    yabd_ref[:, gw:2 * gw] = (bg * conv).astype(BF16)

    ga = proj(7)
    gg = proj(8)
    ext_ref[HALO:, 2 * gw:3 * gw] = ga * jax.nn.sigmoid(gg)
    n_c = confw_ref.shape[0]
    u = confb_ref[...] + confw_ref[n_c - 1:n_c, :] * hist(2, 0)
    for j in range(n_c - 1):
        u = u + confw_ref[j:j + 1, :] * hist(2, n_c - 1 - j)
    mu = jnp.mean(u, axis=-1, keepdims=True)
    uc = u - mu
    ln = uc * lax.rsqrt(jnp.mean(uc * uc, axis=-1, keepdims=True) + EPS) * lng_ref[...] + lnb_ref[...]
    yabd_ref[:, 2 * gw:3 * gw] = (ln * jax.nn.sigmoid(ln)).astype(BF16)

    ext_ref[0:HALO, :] = ext_ref[tm:tm + HALO, :]

    lane_q = lax.broadcasted_iota(jnp.int32, (tm, gw), 1)
    first_half = (lane_q % dk) < (rot // 2)
    reps = gw // V7X_LANES
    cos_t = jnp.concatenate([ct_ref[...]] * reps, axis=1)
    sin_t = jnp.concatenate([st_ref[...]] * reps, axis=1)

    def rope(v):
        partner = jnp.where(first_half,
                            pltpu.roll(v, gw - rot // 2, axis=1),
                            pltpu.roll(v, rot // 2, axis=1))
        return v * cos_t + partner * sin_t

    qt_ref[...] = rope(proj(4)).T.astype(BF16)
    k_ref[...] = rope(proj(5)).astype(BF16)
    vt_ref[...] = proj(6).T.astype(BF16)


def _mix_in(x, layer, b, s, tm, pre_g, w_in, ct, st, poolw, pscale, sconv_w, conf_w,
            conf_b, ln_g, ln_b):
    n, d = x.shape
    gw = d // N_MIXERS
    in_cols = w_in.shape[-1]
    dk = gw // DIFF_HEADS // 2
    rot = dk // 4
    nt = s // tm
    row = lambda bi, ti: (bi * nt + ti, 0)
    const = lambda bi, ti: (layer, 0, 0)
    est = 2 * d * in_cols * 2 + 4 * tm * d * 4 + (HALO + tm) * 3 * gw * 4 + 12 * tm * gw * 4
    kern = functools.partial(_mix_in_kernel, gw=gw, dk=dk, rot=rot)
    return pl.pallas_call(
        kern,
        grid=(b, nt),
        in_specs=[
            pl.BlockSpec((tm, d), row),
            pl.BlockSpec((None, 1, d), const),
            pl.BlockSpec((None, d, in_cols), const),
            pl.BlockSpec((tm, V7X_LANES), row),
            pl.BlockSpec((tm, V7X_LANES), row),
            pl.BlockSpec((None, gw, gw), const),
            pl.BlockSpec((None, 1, gw), const),
            pl.BlockSpec((None,) + sconv_w.shape[1:], const),
            pl.BlockSpec((None,) + conf_w.shape[1:], const),
            pl.BlockSpec((None, 1, gw), const),
            pl.BlockSpec((None, 1, gw), const),
            pl.BlockSpec((None, 1, gw), const),
        ],
        out_specs=[
            pl.BlockSpec((tm, 3 * gw), row),
            pl.BlockSpec((None, gw, tm), lambda bi, ti: (bi, 0, ti)),
            pl.BlockSpec((None, tm, gw), lambda bi, ti: (bi, ti, 0)),
            pl.BlockSpec((None, gw, tm), lambda bi, ti: (bi, 0, ti)),
        ],
        out_shape=[
            jax.ShapeDtypeStruct((n, 3 * gw), BF16),
            jax.ShapeDtypeStruct((b, gw, s), BF16),
            jax.ShapeDtypeStruct((b, s, gw), BF16),
            jax.ShapeDtypeStruct((b, gw, s), BF16),
        ],
        scratch_shapes=[pltpu.VMEM((HALO + tm, 3 * gw), F32)],
        compiler_params=pltpu.CompilerParams(
            dimension_semantics=("arbitrary", "arbitrary"), vmem_limit_bytes=_vmem_limit(est)),
        name="mix_in",
    )(x, pre_g, w_in, ct, st, poolw, pscale, sconv_w, conf_w, conf_b, ln_g, ln_b)


def _diff_attn_kernel(qt_ref, k_ref, vt_ref, yabd_ref, x_ref, wo_ref, lam_ref, subg_ref,
                      post_g_ref, o_ref, qs_ref, acc_ref, *, dk):
    gw, tq = qt_ref.shape
    tk = tq
    dv = 2 * dk
    n_hc = gw // dk
    qi = pl.program_id(1)
    c_exp = dk ** -0.5 * LOG2E

    for hc in range(n_hc):
        qs_ref[hc] = jnp.zeros((gw, tq), BF16)
        qs_ref[hc, hc * dk:(hc + 1) * dk, :] = qt_ref[hc * dk:(hc + 1) * dk, :]
    acc_ref[...] = jnp.zeros(acc_ref.shape, F32)

    def kv_step(j, carry, masked):
        m_prev, l_prev = carry
        off = pl.multiple_of(j * tk, tk)
        kb = k_ref[pl.ds(off, tk), :]
        if masked:
            kchunk = lax.broadcasted_iota(jnp.int32, (tk, tq), 0) // CHUNK
            qchunk = lax.broadcasted_iota(jnp.int32, (tk, tq), 1) // CHUNK
            visible = kchunk <= qchunk
        m_out, l_out = [], []
        for hc in range(n_hc):
            head = hc // 2
            sc = _dot(kb, qs_ref[hc])
            if masked:
                sc = jnp.where(visible, sc, NEG_BIG)
            m_new = jnp.maximum(m_prev[hc], jnp.max(sc, axis=0, keepdims=True))
            alpha = jnp.exp2((m_prev[hc] - m_new) * c_exp)
            p = jnp.exp2((sc - m_new) * c_exp)
            l_out.append(alpha * l_prev[hc] + jnp.sum(p, axis=0, keepdims=True))
            m_out.append(m_new)
            vt = vt_ref[head * dv:(head + 1) * dv, pl.ds(off, tk)]
            acc_ref[hc] = alpha * acc_ref[hc] + _dot(vt, p.astype(BF16))
        return tuple(m_out), tuple(l_out)

    init = (tuple(jnp.full((1, tq), NEG_BIG, F32) for _ in range(n_hc)),
            tuple(jnp.zeros((1, tq), F32) for _ in range(n_hc)))
    carry = lax.fori_loop(0, qi, functools.partial(kv_step, masked=False), init)
    _, l_fin = kv_step(qi, carry, masked=True)

    lam_init = lam_ref[4:5, 0:1]
    lam = (jnp.exp(jnp.sum(lam_ref[0:1, :] * lam_ref[1:2, :], axis=1, keepdims=True))
           - jnp.exp(jnp.sum(lam_ref[2:3, :] * lam_ref[3:4, :], axis=1, keepdims=True))
           + lam_init)
    heads = []
    for head in range(n_hc // 2):
        o = acc_ref[2 * head] / l_fin[2 * head] - lam * (acc_ref[2 * head + 1] / l_fin[2 * head + 1])
        ms = jnp.mean(o * o, axis=0, keepdims=True)
        heads.append(o * lax.rsqrt(ms + EPS) * subg_ref[head * dv:(head + 1) * dv, :] * (1.0 - lam_init))
    yc = jnp.concatenate(heads, axis=0).T.astype(BF16)

    y = (_dot(yabd_ref[:, 0:2 * gw], wo_ref[0:2 * gw, :])
         + _dot(yc, wo_ref[2 * gw:3 * gw, :])
         + _dot(yabd_ref[:, 2 * gw:3 * gw], wo_ref[3 * gw:4 * gw, :]))
    o_ref[...] = x_ref[...] + _rms(y, post_g_ref[...])


def _diff_attn(x, layer, b, s, qt, k, vt, yabd, wo, lamv, subg, post_g):
    n, d = x.shape
    gw = d // N_MIXERS
    dk = gw // DIFF_HEADS // 2
    tq = V7X_MXU_DIM
    nq = s // tq
    row = lambda bi, qi: (bi * nq + qi, 0)
    const = lambda bi, qi: (layer, 0, 0)
    est = (2 * 2 * s * gw * 2 + 2 * d * d * 2 + 4 * tq * d * 4 + 2 * gw * tq * 2 * (gw // dk)
           + 16 * tq * tq * 4)
    kern = functools.partial(_diff_attn_kernel, dk=dk)
    return pl.pallas_call(
        kern,
        grid=(b, nq),
        in_specs=[
            pl.BlockSpec((None, gw, tq), lambda bi, qi: (bi, 0, qi)),
            pl.BlockSpec((None, s, gw), lambda bi, qi: (bi, 0, 0)),
            pl.BlockSpec((None, gw, s), lambda bi, qi: (bi, 0, 0)),
            pl.BlockSpec((tq, 3 * gw), row),
            pl.BlockSpec((tq, d), row),
            pl.BlockSpec((None, d, d), const),
            pl.BlockSpec((None,) + lamv.shape[1:], const),
            pl.BlockSpec((None, gw, 1), const),
            pl.BlockSpec((None, 1, d), const),
        ],
        out_specs=pl.BlockSpec((tq, d), row),
        out_shape=jax.ShapeDtypeStruct((n, d), F32),
        scratch_shapes=[pltpu.VMEM((gw // dk, gw, tq), BF16),
                        pltpu.VMEM((gw // dk, 2 * dk, tq), F32)],
        compiler_params=pltpu.CompilerParams(
            dimension_semantics=("arbitrary", "arbitrary"), vmem_limit_bytes=_vmem_limit(est)),
        name="diff_attn",
    )(qt, k, vt, yabd, x, wo, lamv, subg, post_g)


def _mem_proj_kernel(mem_ref, g_ref, wk_ref, wv_ref, kt_ref, v_ref):
    mn = _rms(mem_ref[...], g_ref[...]).astype(BF16)
    kt_ref[...] = _dot(mn, wk_ref[...]).T.astype(BF16)
    v_ref[...] = _dot(mn, wv_ref[...]).astype(BF16)


def _mem_proj(mem, layer, g, wk, wv):
    b, m, d = mem.shape
    const = lambda bi: (layer, 0, 0)
    est = 2 * 2 * d * d * 2 + 8 * m * d * 4
    return pl.pallas_call(
        _mem_proj_kernel,
        grid=(b,),
        in_specs=[
            pl.BlockSpec((None, m, d), lambda bi: (bi, 0, 0)),
            pl.BlockSpec((None, 1, d), const),
            pl.BlockSpec((None, d, d), const),
            pl.BlockSpec((None, d, d), const),
        ],
        out_specs=[
            pl.BlockSpec((None, d, m), lambda bi: (bi, 0, 0)),
            pl.BlockSpec((None, m, d), lambda bi: (bi, 0, 0)),
        ],
        out_shape=[jax.ShapeDtypeStruct((b, d, m), BF16), jax.ShapeDtypeStruct((b, m, d), BF16)],
        compiler_params=pltpu.CompilerParams(
            dimension_semantics=("arbitrary",), vmem_limit_bytes=_vmem_limit(est)),
        name="mem_proj",
    )(mem, g, wk, wv)


def _xattn_kernel(x_ref, pre_g_ref, wq_ref, kt_ref, v_ref, wo_ref, post_g_ref, o_ref):
    x = x_ref[...]
    d = x.shape[1]
    hd = d // X_HEADS
    h = _rms(x, pre_g_ref[...]).astype(BF16)
    outs = []
    for head in range(X_HEADS):
        cols = slice(head * hd, (head + 1) * hd)
        qh = _dot(h, wq_ref[:, cols]).astype(BF16)
        sc = _dot(qh, kt_ref[cols, :]) * (hd ** -0.5)
        p = jnp.exp(sc - jnp.max(sc, axis=-1, keepdims=True))
        l = jnp.sum(p, axis=-1, keepdims=True)
        outs.append(_dot(p.astype(BF16), v_ref[:, cols]) / l)
    ox = jnp.concatenate(outs, axis=1).astype(BF16)
    o_ref[...] = x + _rms(_dot(ox, wo_ref[...]), post_g_ref[...])


def _xattn(x, layer, s, tm, pre_g, wq, kt, v, wo, post_g):
    n, d = x.shape
    m = v.shape[1]
    nt = s // tm
    const = lambda i: (layer, 0, 0)
    est = 2 * 2 * d * d * 2 + 4 * d * m * 2 + 8 * tm * d * 4
    return pl.pallas_call(
        _xattn_kernel,
        grid=(n // tm,),
        in_specs=[
            pl.BlockSpec((tm, d), lambda i: (i, 0)),
            pl.BlockSpec((None, 1, d), const),
            pl.BlockSpec((None, d, d), const),
            pl.BlockSpec((None, d, m), lambda i: (i // nt, 0, 0)),
            pl.BlockSpec((None, m, d), lambda i: (i // nt, 0, 0)),
            pl.BlockSpec((None, d, d), const),
            pl.BlockSpec((None, 1, d), const),
        ],
        out_specs=pl.BlockSpec((tm, d), lambda i: (i, 0)),
        out_shape=jax.ShapeDtypeStruct((n, d), F32),
        compiler_params=pltpu.CompilerParams(
            dimension_semantics=("arbitrary",), vmem_limit_bytes=_vmem_limit(est)),
        name="xattn",
    )(x, pre_g, wq, kt, v, wo, post_g)


def _rope_tables(positions, dk, rot):
    inv_freq = ROPE_THETA ** (-jnp.arange(0, rot, 2, dtype=F32) / rot)
    ang = positions.astype(F32)[..., None] * inv_freq
    cos, sin = jnp.cos(ang), jnp.sin(ang)
    pad = cos.shape[:-1] + (dk - rot,)
    c = jnp.concatenate([cos, cos, jnp.ones(pad, F32)], axis=-1)
    sn = jnp.concatenate([-sin, sin, jnp.zeros(pad, F32)], axis=-1)
    reps = V7X_LANES // dk
    n = positions.size
    return (jnp.tile(c, (1, 1, reps)).reshape(n, V7X_LANES),
            jnp.tile(sn, (1, 1, reps)).reshape(n, V7X_LANES))


def kernel(x, mem, positions, f1_pre_g, f1_w_gate, f1_w_up, f1_w_down, f1_post_g, mix_pre_g, w_in, pool_w, pool_scale, sconv_w, lam_q1, lam_k1, lam_q2, lam_k2, diff_subln_g, conf_dw_w, conf_dw_b, conf_ln_g, conf_ln_b, w_o, mix_post_g, x_pre_g, x_mem_g, x_wq, x_wk, x_wv, x_wo, x_post_g, f2_pre_g, f2_w_gate, f2_w_up, f2_w_down, f2_post_g):
    b, s, d = x.shape
    depth = w_in.shape[0]
    gw = d // N_MIXERS
    dk = gw // DIFF_HEADS // 2
    rot = dk // 4
    tm = _row_tile(s)
    assert s % V7X_MXU_DIM == 0 and V7X_MXU_DIM % CHUNK == 0 and tm >= HALO
    assert HALO >= conf_dw_w.shape[1] - 1 and HALO >= max(POOL_WINDOWS) - 1

    bf = lambda w: w.astype(BF16)
    row = lambda g: g[:, None, :]
    ct, st = _rope_tables(positions, dk, rot)
    n_pg = pool_w.shape[1]
    poolw = bf(jnp.einsum("lgcd,gh->lgchd", pool_w, jnp.eye(n_pg, dtype=F32)).reshape(depth, gw, gw))
    lam_init = jnp.asarray([0.8 - 0.6 * math.exp(-0.3 * l) for l in range(depth)], F32)
    lamv = jnp.concatenate(
        [jnp.stack([lam_q1, lam_k1, lam_q2, lam_k2], axis=1).astype(F32),
         jnp.broadcast_to(lam_init[:, None, None], (depth, 1, dk)),
         jnp.zeros((depth, 3, dk), F32)], axis=1)
    subg = jnp.tile(diff_subln_g, (1, DIFF_HEADS))[:, :, None]
    f1 = (row(f1_pre_g), bf(f1_w_gate), bf(f1_w_up), bf(f1_w_down), row(f1_post_g))
    f2 = (row(f2_pre_g), bf(f2_w_gate), bf(f2_w_up), bf(f2_w_down), row(f2_post_g))
    w_in_b, w_o_b = bf(w_in), bf(w_o)
    wq_b, wk_b, wv_b, wo_b = bf(x_wq), bf(x_wk), bf(x_wv), bf(x_wo)

    xf = x.reshape(b * s, d)
    for l in range(depth):
        xf = _ffn(xf, l, *f1, tm)
        yabd, qt, k, vt = _mix_in(xf, l, b, s, tm, row(mix_pre_g), w_in_b, ct, st, poolw,
                                  row(pool_scale), sconv_w, conf_dw_w, row(conf_dw_b),
                                  row(conf_ln_g), row(conf_ln_b))
        xf = _diff_attn(xf, l, b, s, qt, k, vt, yabd, w_o_b, lamv, subg, row(mix_post_g))
        kt, v = _mem_proj(mem, l, row(x_mem_g), wk_b, wv_b)
        xf = _xattn(xf, l, s, tm, row(x_pre_g), wq_b, kt, v, wo_b, row(x_post_g))
        xf = _ffn(xf, l, *f2, tm)
    return xf.reshape(b, s, d)
```

```python
import functools
import math

import jax
import jax.numpy as jnp
from jax import lax
from jax.experimental import pallas as pl
from jax.experimental.pallas import tpu as pltpu

F32 = jnp.float32
BF16 = jnp.bfloat16

EPS = 1e-6
CHUNK = 64
N_MIXERS = 4
POOL_WINDOWS = (2, 4, 8, 16)
DIFF_HEADS = 4
ROPE_THETA = 500000.0
X_HEADS = 4

V7X_LANES = 128
V7X_SUBLANES = 8
V7X_MXU_DIM = 256
V7X_VMEM_BYTES = 64 * 1024 * 1024

HALO = 32
NEG_BIG = -1e30
LOG2E = math.log2(math.e)


def _vmem_limit(estimate_bytes):
    return int(min(estimate_bytes + (12 << 20), V7X_VMEM_BYTES - (6 << 20)))


def _rms(x, g):
    return x * lax.rsqrt(jnp.mean(x * x, axis=-1, keepdims=True) + EPS) * g


def _dot(a, b):
    return jnp.dot(a, b, preferred_element_type=F32)


def _row_tile(rows_per_seq):
    for cand in (512, 256, 128):
        if rows_per_seq % cand == 0:
            return cand
    raise ValueError(f"sequence length {rows_per_seq} must be a multiple of 128")


def _ffn_kernel(x_ref, pre_g_ref, wg_ref, wu_ref, wd_ref, post_g_ref, o_ref):
    x = x_ref[...]
    h = _rms(x, pre_g_ref[...]).astype(BF16)
    g = _dot(h, wg_ref[...])
    u = _dot(h, wu_ref[...])
    a = (g * jax.nn.sigmoid(g) * u).astype(BF16)
    y = _dot(a, wd_ref[...])
    o_ref[...] = x + 0.5 * _rms(y, post_g_ref[...])


def _ffn(x, layer, pre_g, wg, wu, wd, post_g, tm):
    n, d = x.shape
    f = wg.shape[-1]
    const = lambda i: (layer, 0, 0)
    est = 3 * d * f * 2 * 2 + 4 * tm * d * 4 + 3 * tm * f * 4
    return pl.pallas_call(
        _ffn_kernel,
        grid=(n // tm,),
        in_specs=[
            pl.BlockSpec((tm, d), lambda i: (i, 0)),
            pl.BlockSpec((None, 1, d), const),
            pl.BlockSpec((None, d, f), const),
            pl.BlockSpec((None, d, f), const),
            pl.BlockSpec((None, f, d), const),
            pl.BlockSpec((None, 1, d), const),
        ],
        out_specs=pl.BlockSpec((tm, d), lambda i: (i, 0)),
        out_shape=jax.ShapeDtypeStruct((n, d), F32),
        compiler_params=pltpu.CompilerParams(
            dimension_semantics=("arbitrary",), vmem_limit_bytes=_vmem_limit(est)),
        name="ffn",
    )(x, pre_g, wg, wu, wd, post_g)


def _mix_in_kernel(x_ref, g_ref, w_in_ref, ct_ref, st_ref, poolw_ref, pscale_ref,
                   sconv_ref, confw_ref, confb_ref, lng_ref, lnb_ref,
                   yabd_ref, qt_ref, k_ref, vt_ref, ext_ref, *, gw, dk, rot):
    tm = x_ref.shape[0]
    t_idx = pl.program_id(1)

    @pl.when(t_idx == 0)
    def _():
        ext_ref[0:HALO, :] = jnp.zeros((HALO, ext_ref.shape[1]), F32)

    x = x_ref[...]
    h = _rms(x, g_ref[...]).astype(BF16)

    def proj(c):
        return _dot(h, w_in_ref[:, c * gw:(c + 1) * gw])

    def hist(col, shift):
        return ext_ref[HALO - shift:HALO - shift + tm, col * gw:(col + 1) * gw]

    za = proj(0)
    ext_ref[HALO:, 0:gw] = za
    pool_ch = gw // len(POOL_WINDOWS)
    frame = lax.broadcasted_iota(jnp.int32, (tm, V7X_LANES), 0) + t_idx * tm
    lane = lax.broadcasted_iota(jnp.int32, (tm, V7X_LANES), 1)
    halves = []
    for hv in range(gw // V7X_LANES):
        groups = [g for g in range(len(POOL_WINDOWS)) if (g * pool_ch) // V7X_LANES == hv]
        wins = [POOL_WINDOWS[g] for g in groups]
        lo = hv * V7X_LANES
        run = ext_ref[HALO:HALO + tm, lo:lo + V7X_LANES]
        cur = run
        prefix = {1: run}
        for j in range(1, max(wins)):
            run = run + ext_ref[HALO - j:HALO - j + tm, lo:lo + V7X_LANES]
            prefix[j + 1] = run
        csum = prefix[wins[-1]]
        win = jnp.full((tm, V7X_LANES), wins[-1], jnp.int32)
        for g, w in zip(groups[:-1][::-1], wins[:-1][::-1]):
            sel = lane < (g + 1) * pool_ch - lo
            csum = jnp.where(sel, prefix[w], csum)
            win = jnp.where(sel, w, win)
        cnt = jnp.minimum(frame + 1, win).astype(F32)
        halves.append(csum / cnt - cur)
    mixed = jnp.concatenate(halves, axis=1).astype(BF16)
    ya = _dot(mixed, poolw_ref[...]) * pscale_ref[...]
    yabd_ref[:, 0:gw] = ya.astype(BF16)

    xs = proj(1)
    bg = proj(2)
    cg = proj(3)
    ext_ref[HALO:, gw:2 * gw] = cg * xs
    n_s = sconv_ref.shape[0]
    conv = sconv_ref[n_s - 1:n_s, :] * hist(1, 0)
    for j in range(n_s - 1):
        conv = conv + sconv_ref[j:j + 1, :] * hist(1, n_s - 1 - j)
    yabd_ref[:, gw:2 * gw] = (bg * conv).astype(BF16)

    ga = proj(7)
    gg = proj(8)
    ext_ref[HALO:, 2 * gw:3 * gw] = ga * jax.nn.sigmoid(gg)
    n_c = confw_ref.shape[0]
    u = confb_ref[...] + confw_ref[n_c - 1:n_c, :] * hist(2, 0)
    for j in range(n_c - 1):
        u = u + confw_ref[j:j + 1, :] * hist(2, n_c - 1 - j)
    mu = jnp.mean(u, axis=-1, keepdims=True)
    uc = u - mu
    ln = uc * lax.rsqrt(jnp.mean(uc * uc, axis=-1, keepdims=True) + EPS) * lng_ref[...] + lnb_ref[...]
    yabd_ref[:, 2 * gw:3 * gw] = (ln * jax.nn.sigmoid(ln)).astype(BF16)

    ext_ref[0:HALO, :] = ext_ref[tm:tm + HALO, :]

    lane_q = lax.broadcasted_iota(jnp.int32, (tm, gw), 1)
    first_half = (lane_q % dk) < (rot // 2)
    reps = gw // V7X_LANES
    cos_t = jnp.concatenate([ct_ref[...]] * reps, axis=1)
    sin_t = jnp.concatenate([st_ref[...]] * reps, axis=1)

    def rope(v):
        partner = jnp.where(first_half,
                            pltpu.roll(v, gw - rot // 2, axis=1),
                            pltpu.roll(v, rot // 2, axis=1))
        return v * cos_t + partner * sin_t

    qt_ref[...] = rope(proj(4)).T.astype(BF16)
    k_ref[...] = rope(proj(5)).astype(BF16)
    vt_ref[...] = proj(6).T.astype(BF16)


def _mix_in(x, layer, b, s, tm, pre_g, w_in, ct, st, poolw, pscale, sconv_w, conf_w,
            conf_b, ln_g, ln_b):
    n, d = x.shape
    gw = d // N_MIXERS
    in_cols = w_in.shape[-1]
    dk = gw // DIFF_HEADS // 2
    rot = dk // 4
    nt = s // tm
    row = lambda bi, ti: (bi * nt + ti, 0)
    const = lambda bi, ti: (layer, 0, 0)
    est = 2 * d * in_cols * 2 + 4 * tm * d * 4 + (HALO + tm) * 3 * gw * 4 + 12 * tm * gw * 4
    kern = functools.partial(_mix_in_kernel, gw=gw, dk=dk, rot=rot)
    return pl.pallas_call(
        kern,
        grid=(b, nt),
        in_specs=[
            pl.BlockSpec((tm, d), row),
            pl.BlockSpec((None, 1, d), const),
            pl.BlockSpec((None, d, in_cols), const),
            pl.BlockSpec((tm, V7X_LANES), row),
            pl.BlockSpec((tm, V7X_LANES), row),
            pl.BlockSpec((None, gw, gw), const),
            pl.BlockSpec((None, 1, gw), const),
            pl.BlockSpec((None,) + sconv_w.shape[1:], const),
            pl.BlockSpec((None,) + conf_w.shape[1:], const),
            pl.BlockSpec((None, 1, gw), const),
            pl.BlockSpec((None, 1, gw), const),
            pl.BlockSpec((None, 1, gw), const),
        ],
        out_specs=[
            pl.BlockSpec((tm, 3 * gw), row),
            pl.BlockSpec((None, gw, tm), lambda bi, ti: (bi, 0, ti)),
            pl.BlockSpec((None, tm, gw), lambda bi, ti: (bi, ti, 0)),
            pl.BlockSpec((None, gw, tm), lambda bi, ti: (bi, 0, ti)),
        ],
        out_shape=[
            jax.ShapeDtypeStruct((n, 3 * gw), BF16),
            jax.ShapeDtypeStruct((b, gw, s), BF16),
            jax.ShapeDtypeStruct((b, s, gw), BF16),
            jax.ShapeDtypeStruct((b, gw, s), BF16),
        ],
        scratch_shapes=[pltpu.VMEM((HALO + tm, 3 * gw), F32)],
        compiler_params=pltpu.CompilerParams(
            dimension_semantics=("arbitrary", "arbitrary"), vmem_limit_bytes=_vmem_limit(est)),
        name="mix_in",
    )(x, pre_g, w_in, ct, st, poolw, pscale, sconv_w, conf_w, conf_b, ln_g, ln_b)


def _diff_attn_kernel(qt_ref, k_ref, vt_ref, yabd_ref, x_ref, wo_ref, lam_ref, subg_ref,
                      post_g_ref, o_ref, qs_ref, sa_ref, sb_ref, p_ref, m_ref, l_ref, acc_ref,
                      *, dk):
    gw, tq = qt_ref.shape
    tk = tq
    dv = 2 * dk
    n_hc = gw // dk
    n_rc = tk // CHUNK
    qi = pl.program_id(1)
    c_exp = dk ** -0.5 * LOG2E

    for hc in range(n_hc):
        qs_ref[hc] = jnp.zeros((gw, tq), BF16)
        qs_ref[hc, hc * dk:(hc + 1) * dk, :] = qt_ref[hc * dk:(hc + 1) * dk, :]
    acc_ref[...] = jnp.zeros(acc_ref.shape, F32)
    m_ref[...] = jnp.full(m_ref.shape, NEG_BIG, F32)
    l_ref[...] = jnp.zeros(l_ref.shape, F32)

    def fold8(v, op):
        return op(v.reshape(v.shape[0] // V7X_SUBLANES, V7X_SUBLANES, v.shape[1]), axis=0)

    def scores(blk, s_ref):
        kb = k_ref[pl.ds(pl.multiple_of(blk * tk, tk), tk), :]
        for hc in range(n_hc):
            s_ref[hc] = _dot(kb, qs_ref[hc])

    def softmax_pv(blk, s_ref, masked):
        off = pl.multiple_of(blk * tk, tk)
        qchunk = lax.broadcasted_iota(jnp.int32, (1, tq), 1) // CHUNK

        def chunk(hc, r):
            sv = s_ref[hc, r * CHUNK:(r + 1) * CHUNK, :]
            if masked and r > 0:
                sv = jnp.where(qchunk >= r, sv, NEG_BIG)
            return sv

        for hc in range(n_hc):
            head = hc // 2
            cm = fold8(chunk(hc, 0), jnp.max)
            for r in range(1, n_rc):
                cm = jnp.maximum(cm, fold8(chunk(hc, r), jnp.max))
            m_old = m_ref[hc:hc + 1, :]
            m_new = jnp.maximum(m_old, jnp.max(cm, axis=0, keepdims=True))
            alpha = jnp.exp2((m_old - m_new) * c_exp)
            m_ref[hc:hc + 1, :] = m_new
            mc = m_new * c_exp
            ls = None
            for r in range(n_rc):
                p = jnp.exp2(chunk(hc, r) * c_exp - mc)
                ps = fold8(p, jnp.sum)
                ls = ps if ls is None else ls + ps
                p_ref[hc, r * CHUNK:(r + 1) * CHUNK, :] = p.astype(BF16)
            l_ref[hc:hc + 1, :] = alpha * l_ref[hc:hc + 1, :] + jnp.sum(ls, axis=0, keepdims=True)
            vt = vt_ref[head * dv:(head + 1) * dv, pl.ds(off, tk)]
            acc_ref[hc] = alpha * acc_ref[hc] + _dot(vt, p_ref[hc])

    scores(qi, sa_ref)
    softmax_pv(qi, sa_ref, masked=True)

    @pl.when(qi > 0)
    def _():
        scores(0, sa_ref)

        def pair(pi, carry):
            scores(2 * pi + 1, sb_ref)
            softmax_pv(2 * pi, sa_ref, masked=False)
            scores(2 * pi + 2, sa_ref)
            softmax_pv(2 * pi + 1, sb_ref, masked=False)
            return carry

        lax.fori_loop(0, qi // 2, pair, 0)

        @pl.when(qi % 2 == 1)
        def _():
            softmax_pv(qi - 1, sa_ref, masked=False)

    lam_init = lam_ref[4:5, 0:1]
    lam = (jnp.exp(jnp.sum(lam_ref[0:1, :] * lam_ref[1:2, :], axis=1, keepdims=True))
           - jnp.exp(jnp.sum(lam_ref[2:3, :] * lam_ref[3:4, :], axis=1, keepdims=True))
           + lam_init)
    heads = []
    for head in range(n_hc // 2):
        o = (acc_ref[2 * head] / l_ref[2 * head:2 * head + 1, :]
             - lam * (acc_ref[2 * head + 1] / l_ref[2 * head + 1:2 * head + 2, :]))
        ms = jnp.mean(o * o, axis=0, keepdims=True)
        heads.append(o * lax.rsqrt(ms + EPS) * subg_ref[head * dv:(head + 1) * dv, :] * (1.0 - lam_init))
    yc = jnp.concatenate(heads, axis=0).T.astype(BF16)

    y = (_dot(yabd_ref[:, 0:2 * gw], wo_ref[0:2 * gw, :])
         + _dot(yc, wo_ref[2 * gw:3 * gw, :])
         + _dot(yabd_ref[:, 2 * gw:3 * gw], wo_ref[3 * gw:4 * gw, :]))
    o_ref[...] = x_ref[...] + _rms(y, post_g_ref[...])


def _diff_attn(x, layer, b, s, qt, k, vt, yabd, wo, lamv, subg, post_g):
    n, d = x.shape
    gw = d // N_MIXERS
    dk = gw // DIFF_HEADS // 2
    tq = V7X_MXU_DIM
    nq = s // tq
    row = lambda bi, qi: (bi * nq + qi, 0)
    const = lambda bi, qi: (layer, 0, 0)
    n_hc = gw // dk
    est = (2 * 2 * s * gw * 2 + 2 * d * d * 2 + 4 * tq * d * 4 + n_hc * gw * tq * 2
           + n_hc * tq * tq * (4 + 4 + 2) + 8 * tq * d * 4)
    kern = functools.partial(_diff_attn_kernel, dk=dk)
    return pl.pallas_call(
        kern,
        grid=(b, nq),
        in_specs=[
            pl.BlockSpec((None, gw, tq), lambda bi, qi: (bi, 0, qi)),
            pl.BlockSpec((None, s, gw), lambda bi, qi: (bi, 0, 0)),
            pl.BlockSpec((None, gw, s), lambda bi, qi: (bi, 0, 0)),
            pl.BlockSpec((tq, 3 * gw), row),
            pl.BlockSpec((tq, d), row),
            pl.BlockSpec((None, d, d), const),
            pl.BlockSpec((None,) + lamv.shape[1:], const),
            pl.BlockSpec((None, gw, 1), const),
            pl.BlockSpec((None, 1, d), const),
        ],
        out_specs=pl.BlockSpec((tq, d), row),
        out_shape=jax.ShapeDtypeStruct((n, d), F32),
        scratch_shapes=[pltpu.VMEM((n_hc, gw, tq), BF16),
                        pltpu.VMEM((n_hc, tq, tq), F32),
                        pltpu.VMEM((n_hc, tq, tq), F32),
                        pltpu.VMEM((n_hc, tq, tq), BF16),
                        pltpu.VMEM((n_hc, tq), F32),
                        pltpu.VMEM((n_hc, tq), F32),
                        pltpu.VMEM((n_hc, 2 * dk, tq), F32)],
        compiler_params=pltpu.CompilerParams(
            dimension_semantics=("arbitrary", "arbitrary"), vmem_limit_bytes=_vmem_limit(est)),
        name="diff_attn",
    )(qt, k, vt, yabd, x, wo, lamv, subg, post_g)


def _mem_proj_kernel(mem_ref, g_ref, wk_ref, wv_ref, kt_ref, v_ref):
    mn = _rms(mem_ref[...], g_ref[...]).astype(BF16)
    kt_ref[...] = _dot(mn, wk_ref[...]).T.astype(BF16)
    v_ref[...] = _dot(mn, wv_ref[...]).astype(BF16)


def _mem_proj(mem, layer, g, wk, wv):
    b, m, d = mem.shape
    const = lambda bi: (layer, 0, 0)
    est = 2 * 2 * d * d * 2 + 8 * m * d * 4
    return pl.pallas_call(
        _mem_proj_kernel,
        grid=(b,),
        in_specs=[
            pl.BlockSpec((None, m, d), lambda bi: (bi, 0, 0)),
            pl.BlockSpec((None, 1, d), const),
            pl.BlockSpec((None, d, d), const),
            pl.BlockSpec((None, d, d), const),
        ],
        out_specs=[
            pl.BlockSpec((None, d, m), lambda bi: (bi, 0, 0)),
            pl.BlockSpec((None, m, d), lambda bi: (bi, 0, 0)),
        ],
        out_shape=[jax.ShapeDtypeStruct((b, d, m), BF16), jax.ShapeDtypeStruct((b, m, d), BF16)],
        compiler_params=pltpu.CompilerParams(
            dimension_semantics=("arbitrary",), vmem_limit_bytes=_vmem_limit(est)),
        name="mem_proj",
    )(mem, g, wk, wv)


def _xattn_kernel(x_ref, pre_g_ref, wq_ref, kt_ref, v_ref, wo_ref, post_g_ref, o_ref):
    x = x_ref[...]
    d = x.shape[1]
    hd = d // X_HEADS
    h = _rms(x, pre_g_ref[...]).astype(BF16)
    outs = []
    for head in range(X_HEADS):
        cols = slice(head * hd, (head + 1) * hd)
        qh = _dot(h, wq_ref[:, cols]).astype(BF16)
        sc = _dot(qh, kt_ref[cols, :]) * (hd ** -0.5)
        p = jnp.exp(sc - jnp.max(sc, axis=-1, keepdims=True))
        l = jnp.sum(p, axis=-1, keepdims=True)
        outs.append(_dot(p.astype(BF16), v_ref[:, cols]) / l)
    ox = jnp.concatenate(outs, axis=1).astype(BF16)
    o_ref[...] = x + _rms(_dot(ox, wo_ref[...]), post_g_ref[...])


def _xattn(x, layer, s, tm, pre_g, wq, kt, v, wo, post_g):
    n, d = x.shape
    m = v.shape[1]
    nt = s // tm
    const = lambda i: (layer, 0, 0)
    est = 2 * 2 * d * d * 2 + 4 * d * m * 2 + 8 * tm * d * 4
    return pl.pallas_call(
        _xattn_kernel,
        grid=(n // tm,),
        in_specs=[
            pl.BlockSpec((tm, d), lambda i: (i, 0)),
            pl.BlockSpec((None, 1, d), const),
            pl.BlockSpec((None, d, d), const),
            pl.BlockSpec((None, d, m), lambda i: (i // nt, 0, 0)),
            pl.BlockSpec((None, m, d), lambda i: (i // nt, 0, 0)),
            pl.BlockSpec((None, d, d), const),
            pl.BlockSpec((None, 1, d), const),
        ],
        out_specs=pl.BlockSpec((tm, d), lambda i: (i, 0)),
        out_shape=jax.ShapeDtypeStruct((n, d), F32),
        compiler_params=pltpu.CompilerParams(
            dimension_semantics=("arbitrary",), vmem_limit_bytes=_vmem_limit(est)),
        name="xattn",
    )(x, pre_g, wq, kt, v, wo, post_g)


def _rope_tables(positions, dk, rot):
    inv_freq = ROPE_THETA ** (-jnp.arange(0, rot, 2, dtype=F32) / rot)
    ang = positions.astype(F32)[..., None] * inv_freq
    cos, sin = jnp.cos(ang), jnp.sin(ang)
    pad = cos.shape[:-1] + (dk - rot,)
    c = jnp.concatenate([cos, cos, jnp.ones(pad, F32)], axis=-1)
    sn = jnp.concatenate([-sin, sin, jnp.zeros(pad, F32)], axis=-1)
    reps = V7X_LANES // dk
    n = positions.size
    return (jnp.tile(c, (1, 1, reps)).reshape(n, V7X_LANES),
            jnp.tile(sn, (1, 1, reps)).reshape(n, V7X_LANES))


def kernel(x, mem, positions, f1_pre_g, f1_w_gate, f1_w_up, f1_w_down, f1_post_g, mix_pre_g, w_in, pool_w, pool_scale, sconv_w, lam_q1, lam_k1, lam_q2, lam_k2, diff_subln_g, conf_dw_w, conf_dw_b, conf_ln_g, conf_ln_b, w_o, mix_post_g, x_pre_g, x_mem_g, x_wq, x_wk, x_wv, x_wo, x_post_g, f2_pre_g, f2_w_gate, f2_w_up, f2_w_down, f2_post_g):
    b, s, d = x.shape
    depth = w_in.shape[0]
    gw = d // N_MIXERS
    dk = gw // DIFF_HEADS // 2
    rot = dk // 4
    tm = _row_tile(s)
    assert s % V7X_MXU_DIM == 0 and V7X_MXU_DIM % CHUNK == 0 and tm >= HALO
    assert HALO >= conf_dw_w.shape[1] - 1 and HALO >= max(POOL_WINDOWS) - 1

    bf = lambda w: w.astype(BF16)
    row = lambda g: g[:, None, :]
    ct, st = _rope_tables(positions, dk, rot)
    n_pg = pool_w.shape[1]
    poolw = bf(jnp.einsum("lgcd,gh->lgchd", pool_w, jnp.eye(n_pg, dtype=F32)).reshape(depth, gw, gw))
    lam_init = jnp.asarray([0.8 - 0.6 * math.exp(-0.3 * l) for l in range(depth)], F32)
    lamv = jnp.concatenate(
        [jnp.stack([lam_q1, lam_k1, lam_q2, lam_k2], axis=1).astype(F32),
         jnp.broadcast_to(lam_init[:, None, None], (depth, 1, dk)),
         jnp.zeros((depth, 3, dk), F32)], axis=1)
    subg = jnp.tile(diff_subln_g, (1, DIFF_HEADS))[:, :, None]
    f1 = (row(f1_pre_g), bf(f1_w_gate), bf(f1_w_up), bf(f1_w_down), row(f1_post_g))
    f2 = (row(f2_pre_g), bf(f2_w_gate), bf(f2_w_up), bf(f2_w_down), row(f2_post_g))
    w_in_b, w_o_b = bf(w_in), bf(w_o)
    wq_b, wk_b, wv_b, wo_b = bf(x_wq), bf(x_wk), bf(x_wv), bf(x_wo)

    xf = x.reshape(b * s, d)
    for l in range(depth):
        xf = _ffn(xf, l, *f1, tm)
        yabd, qt, k, vt = _mix_in(xf, l, b, s, tm, row(mix_pre_g), w_in_b, ct, st, poolw,
                                  row(pool_scale), sconv_w, conf_dw_w, row(conf_dw_b),
                                  row(conf_ln_g), row(conf_ln_b))
        xf = _diff_attn(xf, l, b, s, qt, k, vt, yabd, w_o_b, lamv, subg, row(mix_post_g))
        kt, v = _mem_proj(mem, l, row(x_mem_g), wk_b, wv_b)
        xf = _xattn(xf, l, s, tm, row(x_pre_g), wq_b, kt, v, wo_b, row(x_post_g))
        xf = _ffn(xf, l, *f2, tm)
    return xf.reshape(b, s, d)
```

```python
import functools
import math

import jax
import jax.numpy as jnp
from jax import lax
from jax.experimental import pallas as pl
from jax.experimental.pallas import tpu as pltpu

F32 = jnp.float32
BF16 = jnp.bfloat16

EPS = 1e-6
CHUNK = 64
N_MIXERS = 4
POOL_WINDOWS = (2, 4, 8, 16)
DIFF_HEADS = 4
ROPE_THETA = 500000.0
X_HEADS = 4

V7X_LANES = 128
V7X_SUBLANES = 8
V7X_MXU_DIM = 256
V7X_VMEM_BYTES = 64 * 1024 * 1024

V7X_BF16_SUBLANES = 16

HALO = 32
ONES_ROWS = V7X_BF16_SUBLANES
NEG_BIG = -1e30
LOG2E = math.log2(math.e)


def _vmem_limit(estimate_bytes):
    return int(min(estimate_bytes + (12 << 20), V7X_VMEM_BYTES - (6 << 20)))


def _rms(x, g):
    return x * lax.rsqrt(jnp.mean(x * x, axis=-1, keepdims=True) + EPS) * g


def _dot(a, b):
    return jnp.dot(a, b, preferred_element_type=F32)


def _row_tile(rows_per_seq):
    for cand in (512, 256, 128):
        if rows_per_seq % cand == 0:
            return cand
    raise ValueError(f"sequence length {rows_per_seq} must be a multiple of 128")


def _ffn_kernel(x_ref, pre_g_ref, wg_ref, wu_ref, wd_ref, post_g_ref, o_ref):
    x = x_ref[...]
    h = _rms(x, pre_g_ref[...]).astype(BF16)
    g = _dot(h, wg_ref[...])
    u = _dot(h, wu_ref[...])
    a = (g * jax.nn.sigmoid(g) * u).astype(BF16)
    y = _dot(a, wd_ref[...])
    o_ref[...] = x + 0.5 * _rms(y, post_g_ref[...])


def _ffn(x, layer, pre_g, wg, wu, wd, post_g, tm):
    n, d = x.shape
    f = wg.shape[-1]
    const = lambda i: (layer, 0, 0)
    est = 3 * d * f * 2 * 2 + 4 * tm * d * 4 + 3 * tm * f * 4
    return pl.pallas_call(
        _ffn_kernel,
        grid=(n // tm,),
        in_specs=[
            pl.BlockSpec((tm, d), lambda i: (i, 0)),
            pl.BlockSpec((None, 1, d), const),
            pl.BlockSpec((None, d, f), const),
            pl.BlockSpec((None, d, f), const),
            pl.BlockSpec((None, f, d), const),
            pl.BlockSpec((None, 1, d), const),
        ],
        out_specs=pl.BlockSpec((tm, d), lambda i: (i, 0)),
        out_shape=jax.ShapeDtypeStruct((n, d), F32),
        compiler_params=pltpu.CompilerParams(
            dimension_semantics=("arbitrary",), vmem_limit_bytes=_vmem_limit(est)),
        name="ffn",
    )(x, pre_g, wg, wu, wd, post_g)


def _mix_in_kernel(x_ref, g_ref, w_in_ref, ct_ref, st_ref, poolw_ref, pscale_ref,
                   sconv_ref, confw_ref, confb_ref, lng_ref, lnb_ref,
                   yabd_ref, qt_ref, k_ref, vt_ref, ext_ref, *, gw, dk, rot):
    tm = x_ref.shape[0]
    t_idx = pl.program_id(1)

    @pl.when(t_idx == 0)
    def _():
        ext_ref[0:HALO, :] = jnp.zeros((HALO, ext_ref.shape[1]), F32)

    x = x_ref[...]
    h = _rms(x, g_ref[...]).astype(BF16)

    def proj(c):
        return _dot(h, w_in_ref[:, c * gw:(c + 1) * gw])

    def hist(col, shift):
        return ext_ref[HALO - shift:HALO - shift + tm, col * gw:(col + 1) * gw]

    za = proj(0)
    ext_ref[HALO:, 0:gw] = za
    pool_ch = gw // len(POOL_WINDOWS)
    frame = lax.broadcasted_iota(jnp.int32, (tm, V7X_LANES), 0) + t_idx * tm
    lane = lax.broadcasted_iota(jnp.int32, (tm, V7X_LANES), 1)
    halves = []
    for hv in range(gw // V7X_LANES):
        groups = [g for g in range(len(POOL_WINDOWS)) if (g * pool_ch) // V7X_LANES == hv]
        wins = [POOL_WINDOWS[g] for g in groups]
        lo = hv * V7X_LANES
        run = ext_ref[HALO:HALO + tm, lo:lo + V7X_LANES]
        cur = run
        prefix = {1: run}
        for j in range(1, max(wins)):
            run = run + ext_ref[HALO - j:HALO - j + tm, lo:lo + V7X_LANES]
            prefix[j + 1] = run
        csum = prefix[wins[-1]]
        win = jnp.full((tm, V7X_LANES), wins[-1], jnp.int32)
        for g, w in zip(groups[:-1][::-1], wins[:-1][::-1]):
            sel = lane < (g + 1) * pool_ch - lo
            csum = jnp.where(sel, prefix[w], csum)
            win = jnp.where(sel, w, win)
        cnt = jnp.minimum(frame + 1, win).astype(F32)
        halves.append(csum / cnt - cur)
    mixed = jnp.concatenate(halves, axis=1).astype(BF16)
    ya = _dot(mixed, poolw_ref[...]) * pscale_ref[...]
    yabd_ref[:, 0:gw] = ya.astype(BF16)

    xs = proj(1)
    bg = proj(2)
    cg = proj(3)
    ext_ref[HALO:, gw:2 * gw] = cg * xs
    n_s = sconv_ref.shape[0]
    conv = sconv_ref[n_s - 1:n_s, :] * hist(1, 0)
    for j in range(n_s - 1):
        conv = conv + sconv_ref[j:j + 1, :] * hist(1, n_s - 1 - j)
    yabd_ref[:, gw:2 * gw] = (bg * conv).astype(BF16)

    ga = proj(7)
    gg = proj(8)
    ext_ref[HALO:, 2 * gw:3 * gw] = ga * jax.nn.sigmoid(gg)
    n_c = confw_ref.shape[0]
    u = confb_ref[...] + confw_ref[n_c - 1:n_c, :] * hist(2, 0)
    for j in range(n_c - 1):
        u = u + confw_ref[j:j + 1, :] * hist(2, n_c - 1 - j)
    mu = jnp.mean(u, axis=-1, keepdims=True)
    uc = u - mu
    ln = uc * lax.rsqrt(jnp.mean(uc * uc, axis=-1, keepdims=True) + EPS) * lng_ref[...] + lnb_ref[...]
    yabd_ref[:, 2 * gw:3 * gw] = (ln * jax.nn.sigmoid(ln)).astype(BF16)

    ext_ref[0:HALO, :] = ext_ref[tm:tm + HALO, :]

    lane_q = lax.broadcasted_iota(jnp.int32, (tm, gw), 1)
    first_half = (lane_q % dk) < (rot // 2)
    reps = gw // V7X_LANES
    cos_t = jnp.concatenate([ct_ref[...]] * reps, axis=1)
    sin_t = jnp.concatenate([st_ref[...]] * reps, axis=1)

    def rope(v):
        partner = jnp.where(first_half,
                            pltpu.roll(v, gw - rot // 2, axis=1),
                            pltpu.roll(v, rot // 2, axis=1))
        return v * cos_t + partner * sin_t

    qt_ref[...] = (rope(proj(4)) * (dk ** -0.5 * LOG2E)).T.astype(BF16)
    k_ref[...] = rope(proj(5)).astype(BF16)
    vt = proj(6).T.astype(BF16)
    dv = 2 * dk
    for head in range(gw // dv):
        base = head * (dv + ONES_ROWS)
        vt_ref[base:base + dv, :] = vt[head * dv:(head + 1) * dv, :]
        vt_ref[base + dv:base + dv + ONES_ROWS, :] = jnp.ones((ONES_ROWS, tm), BF16)


def _mix_in(x, layer, b, s, tm, pre_g, w_in, ct, st, poolw, pscale, sconv_w, conf_w,
            conf_b, ln_g, ln_b):
    n, d = x.shape
    gw = d // N_MIXERS
    in_cols = w_in.shape[-1]
    dk = gw // DIFF_HEADS // 2
    rot = dk // 4
    nt = s // tm
    vt_rows = DIFF_HEADS * (2 * dk + ONES_ROWS)
    row = lambda bi, ti: (bi * nt + ti, 0)
    const = lambda bi, ti: (layer, 0, 0)
    est = 2 * d * in_cols * 2 + 4 * tm * d * 4 + (HALO + tm) * 3 * gw * 4 + 12 * tm * gw * 4
    kern = functools.partial(_mix_in_kernel, gw=gw, dk=dk, rot=rot)
    return pl.pallas_call(
        kern,
        grid=(b, nt),
        in_specs=[
            pl.BlockSpec((tm, d), row),
            pl.BlockSpec((None, 1, d), const),
            pl.BlockSpec((None, d, in_cols), const),
            pl.BlockSpec((tm, V7X_LANES), row),
            pl.BlockSpec((tm, V7X_LANES), row),
            pl.BlockSpec((None, gw, gw), const),
            pl.BlockSpec((None, 1, gw), const),
            pl.BlockSpec((None,) + sconv_w.shape[1:], const),
            pl.BlockSpec((None,) + conf_w.shape[1:], const),
            pl.BlockSpec((None, 1, gw), const),
            pl.BlockSpec((None, 1, gw), const),
            pl.BlockSpec((None, 1, gw), const),
        ],
        out_specs=[
            pl.BlockSpec((tm, 3 * gw), row),
            pl.BlockSpec((None, gw, tm), lambda bi, ti: (bi, 0, ti)),
            pl.BlockSpec((None, tm, gw), lambda bi, ti: (bi, ti, 0)),
            pl.BlockSpec((None, vt_rows, tm), lambda bi, ti: (bi, 0, ti)),
        ],
        out_shape=[
            jax.ShapeDtypeStruct((n, 3 * gw), BF16),
            jax.ShapeDtypeStruct((b, gw, s), BF16),
            jax.ShapeDtypeStruct((b, s, gw), BF16),
            jax.ShapeDtypeStruct((b, vt_rows, s), BF16),
        ],
        scratch_shapes=[pltpu.VMEM((HALO + tm, 3 * gw), F32)],
        compiler_params=pltpu.CompilerParams(
            dimension_semantics=("arbitrary", "arbitrary"), vmem_limit_bytes=_vmem_limit(est)),
        name="mix_in",
    )(x, pre_g, w_in, ct, st, poolw, pscale, sconv_w, conf_w, conf_b, ln_g, ln_b)


def _diff_attn_kernel(qt_ref, k_ref, vt_ref, yabd_ref, x_ref, wo_ref, lam_ref, subg_ref,
                      post_g_ref, o_ref, qs_ref, sa_ref, sb_ref, cma_ref, cmb_ref, p_ref, m_ref,
                      acc_ref, *, dk, tk, rc):
    gw, tq = qt_ref.shape
    dv = 2 * dk
    dva = dv + ONES_ROWS
    n_hc = gw // dk
    n_rc = tk // rc
    kpq = tq // tk
    qi = pl.program_id(1)

    for hc in range(n_hc):
        qs_ref[hc] = jnp.zeros((gw, tq), BF16)
        qs_ref[hc, hc * dk:(hc + 1) * dk, :] = qt_ref[hc * dk:(hc + 1) * dk, :]
    acc_ref[...] = jnp.zeros(acc_ref.shape, F32)
    m_ref[...] = jnp.full(m_ref.shape, NEG_BIG, F32)

    def fold8(v, op):
        return op(v.reshape(v.shape[0] // V7X_SUBLANES, V7X_SUBLANES, v.shape[1]), axis=0)

    def rows8(hc):
        return slice(hc * V7X_SUBLANES, (hc + 1) * V7X_SUBLANES)

    def scores(blk, s_ref, cm_ref):
        kb = k_ref[pl.ds(pl.multiple_of(blk * tk, tk), tk), :]
        for hc in range(n_hc):
            sc = _dot(kb, qs_ref[hc])
            s_ref[hc] = sc
            if cm_ref is not None:
                cm_ref[rows8(hc), :] = fold8(sc, jnp.max)

    def softmax_pv(blk, s_ref, cm_ref, diag=None):
        off = pl.multiple_of(blk * tk, tk)
        masked = diag is not None
        qchunk = lax.broadcasted_iota(jnp.int32, (1, tq), 1) // CHUNK

        def chunk(hc, r):
            sv = s_ref[hc, r * rc:(r + 1) * rc, :]
            if masked:
                kc = (diag * tk + r * rc) // CHUNK
                if kc > 0:
                    sv = jnp.where(qchunk >= kc, sv, NEG_BIG)
            return sv

        for hc in range(n_hc):
            head = hc // 2
            if masked:
                cm = fold8(chunk(hc, 0), jnp.max)
                for r in range(1, n_rc):
                    cm = jnp.maximum(cm, fold8(chunk(hc, r), jnp.max))
            else:
                cm = cm_ref[rows8(hc), :]
            m_old = m_ref[hc:hc + 1, :]
            m_new = jnp.maximum(m_old, jnp.max(cm, axis=0, keepdims=True))
            alpha = jnp.exp2(m_old - m_new)
            m_ref[hc:hc + 1, :] = m_new
            for r in range(n_rc):
                p_ref[hc, r * rc:(r + 1) * rc, :] = jnp.exp2(chunk(hc, r) - m_new).astype(BF16)
            vt = vt_ref[head * dva:(head + 1) * dva, pl.ds(off, tk)]
            acc_ref[hc] = alpha * acc_ref[hc] + _dot(vt, p_ref[hc])

    assert kpq == 2
    scores(2 * qi, sa_ref, None)
    scores(2 * qi + 1, sb_ref, None)
    softmax_pv(2 * qi, sa_ref, None, diag=0)
    softmax_pv(2 * qi + 1, sb_ref, None, diag=1)

    def pair(pi, carry):
        scores(2 * pi, sa_ref, cma_ref)
        scores(2 * pi + 1, sb_ref, cmb_ref)
        softmax_pv(2 * pi, sa_ref, cma_ref)
        softmax_pv(2 * pi + 1, sb_ref, cmb_ref)
        return carry

    lax.fori_loop(0, qi, pair, 0)

    lam_init = lam_ref[4:5, 0:1]
    lam = (jnp.exp(jnp.sum(lam_ref[0:1, :] * lam_ref[1:2, :], axis=1, keepdims=True))
           - jnp.exp(jnp.sum(lam_ref[2:3, :] * lam_ref[3:4, :], axis=1, keepdims=True))
           + lam_init)
    heads = []
    for head in range(n_hc // 2):
        o = (acc_ref[2 * head, 0:dv, :] / acc_ref[2 * head, dv:dv + 1, :]
             - lam * (acc_ref[2 * head + 1, 0:dv, :] / acc_ref[2 * head + 1, dv:dv + 1, :]))
        ms = jnp.mean(o * o, axis=0, keepdims=True)
        heads.append(o * lax.rsqrt(ms + EPS) * subg_ref[head * dv:(head + 1) * dv, :] * (1.0 - lam_init))
    yc = jnp.concatenate(heads, axis=0).T.astype(BF16)

    y = (_dot(yabd_ref[:, 0:2 * gw], wo_ref[0:2 * gw, :])
         + _dot(yc, wo_ref[2 * gw:3 * gw, :])
         + _dot(yabd_ref[:, 2 * gw:3 * gw], wo_ref[3 * gw:4 * gw, :]))
    o_ref[...] = x_ref[...] + _rms(y, post_g_ref[...])


def _diff_attn(x, layer, b, s, qt, k, vt, yabd, wo, lamv, subg, post_g):
    n, d = x.shape
    gw = d // N_MIXERS
    dk = gw // DIFF_HEADS // 2
    tk = V7X_MXU_DIM
    tq = 2 * tk
    rc = 32
    nq = s // tq
    row = lambda bi, qi: (bi * nq + qi, 0)
    const = lambda bi, qi: (layer, 0, 0)
    n_hc = gw // dk
    est = (2 * s * (gw + vt.shape[1]) * 2 + 2 * d * d * 2 + 4 * tq * d * 4 + n_hc * gw * tq * 2
           + n_hc * tk * tq * (4 + 4 + 2) + 4 * tq * d * 4)
    kern = functools.partial(_diff_attn_kernel, dk=dk, tk=tk, rc=rc)
    return pl.pallas_call(
        kern,
        grid=(b, nq),
        in_specs=[
            pl.BlockSpec((None, gw, tq), lambda bi, qi: (bi, 0, qi)),
            pl.BlockSpec((None, s, gw), lambda bi, qi: (bi, 0, 0)),
            pl.BlockSpec((None, vt.shape[1], s), lambda bi, qi: (bi, 0, 0)),
            pl.BlockSpec((tq, 3 * gw), row),
            pl.BlockSpec((tq, d), row),
            pl.BlockSpec((None, d, d), const),
            pl.BlockSpec((None,) + lamv.shape[1:], const),
            pl.BlockSpec((None, gw, 1), const),
            pl.BlockSpec((None, 1, d), const),
        ],
        out_specs=pl.BlockSpec((tq, d), row),
        out_shape=jax.ShapeDtypeStruct((n, d), F32),
        scratch_shapes=[pltpu.VMEM((n_hc, gw, tq), BF16),
                        pltpu.VMEM((n_hc, tk, tq), F32),
                        pltpu.VMEM((n_hc, tk, tq), F32),
                        pltpu.VMEM((n_hc * V7X_SUBLANES, tq), F32),
                        pltpu.VMEM((n_hc * V7X_SUBLANES, tq), F32),
                        pltpu.VMEM((n_hc, tk, tq), BF16),
                        pltpu.VMEM((n_hc, tq), F32),
                        pltpu.VMEM((n_hc, 2 * dk + ONES_ROWS, tq), F32)],
        compiler_params=pltpu.CompilerParams(
            dimension_semantics=("arbitrary", "arbitrary"), vmem_limit_bytes=_vmem_limit(est)),
        name="diff_attn",
    )(qt, k, vt, yabd, x, wo, lamv, subg, post_g)


def _mem_proj_kernel(mem_ref, g_ref, wk_ref, wv_ref, kt_ref, v_ref):
    mn = _rms(mem_ref[...], g_ref[...]).astype(BF16)
    kt_ref[...] = _dot(mn, wk_ref[...]).T.astype(BF16)
    v_ref[...] = _dot(mn, wv_ref[...]).astype(BF16)


def _mem_proj(mem, layer, g, wk, wv):
    b, m, d = mem.shape
    const = lambda bi: (layer, 0, 0)
    est = 2 * 2 * d * d * 2 + 8 * m * d * 4
    return pl.pallas_call(
        _mem_proj_kernel,
        grid=(b,),
        in_specs=[
            pl.BlockSpec((None, m, d), lambda bi: (bi, 0, 0)),
            pl.BlockSpec((None, 1, d), const),
            pl.BlockSpec((None, d, d), const),
            pl.BlockSpec((None, d, d), const),
        ],
        out_specs=[
            pl.BlockSpec((None, d, m), lambda bi: (bi, 0, 0)),
            pl.BlockSpec((None, m, d), lambda bi: (bi, 0, 0)),
        ],
        out_shape=[jax.ShapeDtypeStruct((b, d, m), BF16), jax.ShapeDtypeStruct((b, m, d), BF16)],
        compiler_params=pltpu.CompilerParams(
            dimension_semantics=("arbitrary",), vmem_limit_bytes=_vmem_limit(est)),
        name="mem_proj",
    )(mem, g, wk, wv)


def _xattn_kernel(x_ref, pre_g_ref, wq_ref, kt_ref, v_ref, wo_ref, post_g_ref, o_ref):
    x = x_ref[...]
    d = x.shape[1]
    hd = d // X_HEADS
    h = _rms(x, pre_g_ref[...]).astype(BF16)
    outs = []
    for head in range(X_HEADS):
        cols = slice(head * hd, (head + 1) * hd)
        qh = _dot(h, wq_ref[:, cols]).astype(BF16)
        sc = _dot(qh, kt_ref[cols, :]) * (hd ** -0.5)
        p = jnp.exp(sc - jnp.max(sc, axis=-1, keepdims=True))
        l = jnp.sum(p, axis=-1, keepdims=True)
        outs.append(_dot(p.astype(BF16), v_ref[:, cols]) / l)
    ox = jnp.concatenate(outs, axis=1).astype(BF16)
    o_ref[...] = x + _rms(_dot(ox, wo_ref[...]), post_g_ref[...])


def _xattn(x, layer, s, tm, pre_g, wq, kt, v, wo, post_g):
    n, d = x.shape
    m = v.shape[1]
    nt = s // tm
    const = lambda i: (layer, 0, 0)
    est = 2 * 2 * d * d * 2 + 4 * d * m * 2 + 8 * tm * d * 4
    return pl.pallas_call(
        _xattn_kernel,
        grid=(n // tm,),
        in_specs=[
            pl.BlockSpec((tm, d), lambda i: (i, 0)),
            pl.BlockSpec((None, 1, d), const),
            pl.BlockSpec((None, d, d), const),
            pl.BlockSpec((None, d, m), lambda i: (i // nt, 0, 0)),
            pl.BlockSpec((None, m, d), lambda i: (i // nt, 0, 0)),
            pl.BlockSpec((None, d, d), const),
            pl.BlockSpec((None, 1, d), const),
        ],
        out_specs=pl.BlockSpec((tm, d), lambda i: (i, 0)),
        out_shape=jax.ShapeDtypeStruct((n, d), F32),
        compiler_params=pltpu.CompilerParams(
            dimension_semantics=("arbitrary",), vmem_limit_bytes=_vmem_limit(est)),
        name="xattn",
    )(x, pre_g, wq, kt, v, wo, post_g)


def _rope_tables(positions, dk, rot):
    inv_freq = ROPE_THETA ** (-jnp.arange(0, rot, 2, dtype=F32) / rot)
    ang = positions.astype(F32)[..., None] * inv_freq
    cos, sin = jnp.cos(ang), jnp.sin(ang)
    pad = cos.shape[:-1] + (dk - rot,)
    c = jnp.concatenate([cos, cos, jnp.ones(pad, F32)], axis=-1)
    sn = jnp.concatenate([-sin, sin, jnp.zeros(pad, F32)], axis=-1)
    reps = V7X_LANES // dk
    n = positions.size
    return (jnp.tile(c, (1, 1, reps)).reshape(n, V7X_LANES),
            jnp.tile(sn, (1, 1, reps)).reshape(n, V7X_LANES))


def kernel(x, mem, positions, f1_pre_g, f1_w_gate, f1_w_up, f1_w_down, f1_post_g, mix_pre_g, w_in, pool_w, pool_scale, sconv_w, lam_q1, lam_k1, lam_q2, lam_k2, diff_subln_g, conf_dw_w, conf_dw_b, conf_ln_g, conf_ln_b, w_o, mix_post_g, x_pre_g, x_mem_g, x_wq, x_wk, x_wv, x_wo, x_post_g, f2_pre_g, f2_w_gate, f2_w_up, f2_w_down, f2_post_g):
    b, s, d = x.shape
    depth = w_in.shape[0]
    gw = d // N_MIXERS
    dk = gw // DIFF_HEADS // 2
    rot = dk // 4
    tm = _row_tile(s)
    assert s % (2 * V7X_MXU_DIM) == 0 and V7X_MXU_DIM % CHUNK == 0 and tm >= HALO
    assert HALO >= conf_dw_w.shape[1] - 1 and HALO >= max(POOL_WINDOWS) - 1

    bf = lambda w: w.astype(BF16)
    row = lambda g: g[:, None, :]
    ct, st = _rope_tables(positions, dk, rot)
    n_pg = pool_w.shape[1]
    poolw = bf(jnp.einsum("lgcd,gh->lgchd", pool_w, jnp.eye(n_pg, dtype=F32)).reshape(depth, gw, gw))
    lam_init = jnp.asarray([0.8 - 0.6 * math.exp(-0.3 * l) for l in range(depth)], F32)
    lamv = jnp.concatenate(
        [jnp.stack([lam_q1, lam_k1, lam_q2, lam_k2], axis=1).astype(F32),
         jnp.broadcast_to(lam_init[:, None, None], (depth, 1, dk)),
         jnp.zeros((depth, 3, dk), F32)], axis=1)
    subg = jnp.tile(diff_subln_g, (1, DIFF_HEADS))[:, :, None]
    f1 = (row(f1_pre_g), bf(f1_w_gate), bf(f1_w_up), bf(f1_w_down), row(f1_post_g))
    f2 = (row(f2_pre_g), bf(f2_w_gate), bf(f2_w_up), bf(f2_w_down), row(f2_post_g))
    w_in_b, w_o_b = bf(w_in), bf(w_o)
    wq_b, wk_b, wv_b, wo_b = bf(x_wq), bf(x_wk), bf(x_wv), bf(x_wo)

    xf = x.reshape(b * s, d)
    for l in range(depth):
        xf = _ffn(xf, l, *f1, tm)
        yabd, qt, k, vt = _mix_in(xf, l, b, s, tm, row(mix_pre_g), w_in_b, ct, st, poolw,
                                  row(pool_scale), sconv_w, conf_dw_w, row(conf_dw_b),
                                  row(conf_ln_g), row(conf_ln_b))
        xf = _diff_attn(xf, l, b, s, qt, k, vt, yabd, w_o_b, lamv, subg, row(mix_post_g))
        kt, v = _mem_proj(mem, l, row(x_mem_g), wk_b, wv_b)
        xf = _xattn(xf, l, s, tm, row(x_pre_g), wq_b, kt, v, wo_b, row(x_post_g))
        xf = _ffn(xf, l, *f2, tm)
    return xf.reshape(b, s, d)
```

```python
import functools
import math

import jax
import jax.numpy as jnp
from jax import lax
from jax.experimental import pallas as pl
from jax.experimental.pallas import tpu as pltpu

F32 = jnp.float32
BF16 = jnp.bfloat16

EPS = 1e-6
CHUNK = 64
N_MIXERS = 4
POOL_WINDOWS = (2, 4, 8, 16)
DIFF_HEADS = 4
ROPE_THETA = 500000.0
X_HEADS = 4

V7X_LANES = 128
V7X_SUBLANES = 8
V7X_MXU_DIM = 256
V7X_VMEM_BYTES = 64 * 1024 * 1024

V7X_BF16_SUBLANES = 16

HALO = 32
ONES_ROWS = V7X_BF16_SUBLANES
MIX_ROWS = 512
CONV_ROWS = 64
XATTN_ROWS = 128
NEG_BIG = -1e30
LOG2E = math.log2(math.e)


def _vmem_limit(estimate_bytes):
    return int(min(estimate_bytes + (12 << 20), V7X_VMEM_BYTES - (6 << 20)))


def _rms(x, g):
    return x * lax.rsqrt(jnp.mean(x * x, axis=-1, keepdims=True) + EPS) * g


def _dot(a, b):
    return jnp.dot(a, b, preferred_element_type=F32)


def _row_tile(rows_per_seq):
    for cand in (512, 256, 128):
        if rows_per_seq % cand == 0:
            return cand
    raise ValueError(f"sequence length {rows_per_seq} must be a multiple of 128")


def _ffn_kernel(x_ref, pre_g_ref, wg_ref, wu_ref, wd_ref, post_g_ref, o_ref):
    x = x_ref[...]
    h = _rms(x, pre_g_ref[...]).astype(BF16)
    g = _dot(h, wg_ref[...])
    u = _dot(h, wu_ref[...])
    a = (g * jax.nn.sigmoid(g) * u).astype(BF16)
    y = _dot(a, wd_ref[...])
    o_ref[...] = x + 0.5 * _rms(y, post_g_ref[...])


def _ffn(x, layer, pre_g, wg, wu, wd, post_g, tm):
    n, d = x.shape
    f = wg.shape[-1]
    const = lambda i: (layer, 0, 0)
    est = 3 * d * f * 2 * 2 + 4 * tm * d * 4 + 3 * tm * f * 4
    return pl.pallas_call(
        _ffn_kernel,
        grid=(n // tm,),
        in_specs=[
            pl.BlockSpec((tm, d), lambda i: (i, 0)),
            pl.BlockSpec((None, 1, d), const),
            pl.BlockSpec((None, d, f), const),
            pl.BlockSpec((None, d, f), const),
            pl.BlockSpec((None, f, d), const),
            pl.BlockSpec((None, 1, d), const),
        ],
        out_specs=pl.BlockSpec((tm, d), lambda i: (i, 0)),
        out_shape=jax.ShapeDtypeStruct((n, d), F32),
        compiler_params=pltpu.CompilerParams(
            dimension_semantics=("arbitrary",), vmem_limit_bytes=_vmem_limit(est)),
        name="ffn",
    )(x, pre_g, wg, wu, wd, post_g)


def _mix_in_kernel(x_ref, g_ref, w_in_ref, ct_ref, st_ref, poolw_ref, pscale_ref,
                   sconv_ref, confw_ref, confb_ref, lng_ref, lnb_ref,
                   yabd_ref, qt_ref, k_ref, vt_ref, ext_ref, dly_ref, *, gw, dk, rot):
    tm = x_ref.shape[0]
    t_idx = pl.program_id(1)
    sub = V7X_SUBLANES
    dv = 2 * dk
    pool_ch = gw // len(POOL_WINDOWS)
    n_s = sconv_ref.shape[0]
    n_c = confw_ref.shape[0]
    reps = gw // V7X_LANES

    @pl.when(t_idx == 0)
    def _():
        ext_ref[0:HALO, :] = jnp.zeros((HALO, ext_ref.shape[1]), F32)

    for s0 in range(0, tm, MIX_ROWS):
        rows = slice(s0, s0 + MIX_ROWS)
        erows = slice(HALO + s0, HALO + s0 + MIX_ROWS)
        h = _rms(x_ref[rows, :], g_ref[...]).astype(BF16)

        def proj(c):
            return _dot(h, w_in_ref[:, c * gw:(c + 1) * gw])

        def hist(col, shift):
            lo = HALO + s0 - shift
            return ext_ref[lo:lo + MIX_ROWS, col * gw:(col + 1) * gw]

        ext_ref[erows, 0:gw] = proj(0)
        frame = lax.broadcasted_iota(jnp.int32, (MIX_ROWS, V7X_LANES), 0) + (t_idx * tm + s0)
        lane = lax.broadcasted_iota(jnp.int32, (MIX_ROWS, V7X_LANES), 1)
        halves = []
        for hv in range(reps):
            groups = [g for g in range(len(POOL_WINDOWS)) if (g * pool_ch) // V7X_LANES == hv]
            wins = [POOL_WINDOWS[g] for g in groups]
            lo = hv * V7X_LANES
            cur = ext_ref[erows, lo:lo + V7X_LANES]
            run = cur
            prefix = {1: run}
            for j in range(1, max(wins)):
                run = run + ext_ref[HALO + s0 - j:HALO + s0 - j + MIX_ROWS, lo:lo + V7X_LANES]
                prefix[j + 1] = run
            csum = prefix[wins[-1]]
            win = jnp.full((MIX_ROWS, V7X_LANES), wins[-1], jnp.int32)
            for g, w in zip(groups[:-1][::-1], wins[:-1][::-1]):
                sel = lane < (g + 1) * pool_ch - lo
                csum = jnp.where(sel, prefix[w], csum)
                win = jnp.where(sel, w, win)
            cnt = jnp.minimum(frame + 1, win).astype(F32)
            halves.append(csum / cnt - cur)
        mixed = jnp.concatenate(halves, axis=1).astype(BF16)
        yabd_ref[rows, 0:gw] = (_dot(mixed, poolw_ref[...]) * pscale_ref[...]).astype(BF16)

        xs = proj(1)
        bg = proj(2)
        cg = proj(3)
        ext_ref[erows, gw:2 * gw] = cg * xs
        conv = sconv_ref[n_s - 1:n_s, :] * hist(1, 0)
        for j in range(n_s - 1):
            conv = conv + sconv_ref[j:j + 1, :] * hist(1, n_s - 1 - j)
        yabd_ref[rows, gw:2 * gw] = (bg * conv).astype(BF16)

        ga = proj(7)
        gg = proj(8)
        ext_ref[erows, 2 * gw:3 * gw] = ga * jax.nn.sigmoid(gg)
        j0 = 0 if s0 == 0 else s0 + HALO - sub
        j1 = s0 + MIX_ROWS + HALO - sub
        for r in range(1, sub):
            dly_ref[r - 1, j0:j1, :] = ext_ref[sub - r + j0:sub - r + j1, 2 * gw:3 * gw]
        for c0 in range(s0, s0 + MIX_ROWS, CONV_ROWS):
            u = confb_ref[...]
            for j in range(n_c):
                a, r = divmod(n_c - 1 - j, sub)
                if r == 0:
                    lo = HALO - sub * a + c0
                    src = ext_ref[lo:lo + CONV_ROWS, 2 * gw:3 * gw]
                else:
                    lo = HALO - sub - sub * a + c0
                    src = dly_ref[r - 1, lo:lo + CONV_ROWS, :]
                u = u + confw_ref[j:j + 1, :] * src
            mu = jnp.mean(u, axis=-1, keepdims=True)
            uc = u - mu
            ln = (uc * lax.rsqrt(jnp.mean(uc * uc, axis=-1, keepdims=True) + EPS) * lng_ref[...]
                  + lnb_ref[...])
            yabd_ref[c0:c0 + CONV_ROWS, 2 * gw:3 * gw] = (ln * jax.nn.sigmoid(ln)).astype(BF16)

        lane_q = lax.broadcasted_iota(jnp.int32, (MIX_ROWS, gw), 1)
        first_half = (lane_q % dk) < (rot // 2)
        cos_t = jnp.concatenate([ct_ref[rows, :]] * reps, axis=1)
        sin_t = jnp.concatenate([st_ref[rows, :]] * reps, axis=1)

        def rope(v):
            partner = jnp.where(first_half,
                                pltpu.roll(v, gw - rot // 2, axis=1),
                                pltpu.roll(v, rot // 2, axis=1))
            return v * cos_t + partner * sin_t

        qt_ref[:, rows] = (rope(proj(4)) * (dk ** -0.5 * LOG2E)).T.astype(BF16)
        k_ref[rows, :] = rope(proj(5)).astype(BF16)
        vt = proj(6).T.astype(BF16)
        for head in range(gw // dv):
            base = head * (dv + ONES_ROWS)
            vt_ref[base:base + dv, rows] = vt[head * dv:(head + 1) * dv, :]
            vt_ref[base + dv:base + dv + ONES_ROWS, rows] = jnp.ones((ONES_ROWS, MIX_ROWS), BF16)

    ext_ref[0:HALO, :] = ext_ref[tm:tm + HALO, :]


def _mix_in(x, layer, b, s, tm, pre_g, w_in, ct, st, poolw, pscale, sconv_w, conf_w,
            conf_b, ln_g, ln_b):
    n, d = x.shape
    gw = d // N_MIXERS
    in_cols = w_in.shape[-1]
    dk = gw // DIFF_HEADS // 2
    rot = dk // 4
    nt = s // tm
    vt_rows = DIFF_HEADS * (2 * dk + ONES_ROWS)
    row = lambda bi, ti: (bi * nt + ti, 0)
    const = lambda bi, ti: (layer, 0, 0)
    est = (2 * d * in_cols * 2 + 4 * tm * d * 4 + (HALO + tm) * (3 + V7X_SUBLANES - 1) * gw * 4
           + 12 * tm * gw * 4)
    kern = functools.partial(_mix_in_kernel, gw=gw, dk=dk, rot=rot)
    return pl.pallas_call(
        kern,
        grid=(b, nt),
        in_specs=[
            pl.BlockSpec((tm, d), row),
            pl.BlockSpec((None, 1, d), const),
            pl.BlockSpec((None, d, in_cols), const),
            pl.BlockSpec((tm, V7X_LANES), row),
            pl.BlockSpec((tm, V7X_LANES), row),
            pl.BlockSpec((None, gw, gw), const),
            pl.BlockSpec((None, 1, gw), const),
            pl.BlockSpec((None,) + sconv_w.shape[1:], const),
            pl.BlockSpec((None,) + conf_w.shape[1:], const),
            pl.BlockSpec((None, 1, gw), const),
            pl.BlockSpec((None, 1, gw), const),
            pl.BlockSpec((None, 1, gw), const),
        ],
        out_specs=[
            pl.BlockSpec((tm, 3 * gw), row),
            pl.BlockSpec((None, gw, tm), lambda bi, ti: (bi, 0, ti)),
            pl.BlockSpec((None, tm, gw), lambda bi, ti: (bi, ti, 0)),
            pl.BlockSpec((None, vt_rows, tm), lambda bi, ti: (bi, 0, ti)),
        ],
        out_shape=[
            jax.ShapeDtypeStruct((n, 3 * gw), BF16),
            jax.ShapeDtypeStruct((b, gw, s), BF16),
            jax.ShapeDtypeStruct((b, s, gw), BF16),
            jax.ShapeDtypeStruct((b, vt_rows, s), BF16),
        ],
        scratch_shapes=[pltpu.VMEM((HALO + tm, 3 * gw), F32),
                        pltpu.VMEM((V7X_SUBLANES - 1, HALO - V7X_SUBLANES + tm, gw), F32)],
        compiler_params=pltpu.CompilerParams(
            dimension_semantics=("arbitrary", "arbitrary"), vmem_limit_bytes=_vmem_limit(est)),
        name="mix_in",
    )(x, pre_g, w_in, ct, st, poolw, pscale, sconv_w, conf_w, conf_b, ln_g, ln_b)


def _diff_attn_kernel(qt_ref, k_ref, vt_ref, yabd_ref, x_ref, wo_ref, lam_ref, subg_ref,
                      post_g_ref, o_ref, qs_ref, sa_ref, sb_ref, cma_ref, cmb_ref, p_ref, m_ref,
                      acc_ref, *, dk, tk, rc):
    gw, tq = qt_ref.shape
    dv = 2 * dk
    dva = dv + ONES_ROWS
    n_hc = gw // dk
    n_rc = tk // rc
    kpq = tq // tk
    qi = pl.program_id(1)

    for hc in range(n_hc):
        qs_ref[hc] = jnp.zeros((gw, tq), BF16)
        qs_ref[hc, hc * dk:(hc + 1) * dk, :] = qt_ref[hc * dk:(hc + 1) * dk, :]
    acc_ref[...] = jnp.zeros(acc_ref.shape, F32)
    m_ref[...] = jnp.full(m_ref.shape, NEG_BIG, F32)

    def fold8(v, op):
        return op(v.reshape(v.shape[0] // V7X_SUBLANES, V7X_SUBLANES, v.shape[1]), axis=0)

    def rows8(hc):
        return slice(hc * V7X_SUBLANES, (hc + 1) * V7X_SUBLANES)

    def scores(blk, s_ref, cm_ref, q0=0):
        kb = k_ref[pl.ds(pl.multiple_of(blk * tk, tk), tk), :]
        for hc in range(n_hc):
            sc = _dot(kb, qs_ref[hc, :, q0:])
            s_ref[hc, :, q0:] = sc
            if cm_ref is not None:
                cm_ref[rows8(hc), q0:] = fold8(sc, jnp.max)

    def softmax_pv(blk, s_ref, cm_ref, diag=None, q0=0):
        off = pl.multiple_of(blk * tk, tk)
        masked = diag is not None
        qchunk = (lax.broadcasted_iota(jnp.int32, (1, tq - q0), 1) + q0) // CHUNK

        def chunk(hc, r):
            sv = s_ref[hc, r * rc:(r + 1) * rc, q0:]
            if masked:
                kc = (diag * tk + r * rc) // CHUNK
                if kc > q0 // CHUNK:
                    sv = jnp.where(qchunk >= kc, sv, NEG_BIG)
            return sv

        for hc in range(n_hc):
            head = hc // 2
            if masked:
                cm = fold8(chunk(hc, 0), jnp.max)
                for r in range(1, n_rc):
                    cm = jnp.maximum(cm, fold8(chunk(hc, r), jnp.max))
            else:
                cm = cm_ref[rows8(hc), q0:]
            m_old = m_ref[hc:hc + 1, q0:]
            m_new = jnp.maximum(m_old, jnp.max(cm, axis=0, keepdims=True))
            alpha = jnp.exp2(m_old - m_new)
            m_ref[hc:hc + 1, q0:] = m_new
            for r in range(n_rc):
                p_ref[hc, r * rc:(r + 1) * rc, q0:] = jnp.exp2(chunk(hc, r) - m_new).astype(BF16)
            vt = vt_ref[head * dva:(head + 1) * dva, pl.ds(off, tk)]
            acc_ref[hc, :, q0:] = alpha * acc_ref[hc, :, q0:] + _dot(vt, p_ref[hc, :, q0:])

    assert kpq == 2
    scores(2 * qi, sa_ref, None)
    scores(2 * qi + 1, sb_ref, None, q0=tk)
    softmax_pv(2 * qi, sa_ref, None, diag=0)
    softmax_pv(2 * qi + 1, sb_ref, None, diag=1, q0=tk)

    def pair(pi, carry):
        scores(2 * pi, sa_ref, cma_ref)
        scores(2 * pi + 1, sb_ref, cmb_ref)
        softmax_pv(2 * pi, sa_ref, cma_ref)
        softmax_pv(2 * pi + 1, sb_ref, cmb_ref)
        return carry

    lax.fori_loop(0, qi, pair, 0)

    lam_init = lam_ref[4:5, 0:1]
    lam = (jnp.exp(jnp.sum(lam_ref[0:1, :] * lam_ref[1:2, :], axis=1, keepdims=True))
           - jnp.exp(jnp.sum(lam_ref[2:3, :] * lam_ref[3:4, :], axis=1, keepdims=True))
           + lam_init)
    heads = []
    for head in range(n_hc // 2):
        o = (acc_ref[2 * head, 0:dv, :] / acc_ref[2 * head, dv:dv + 1, :]
             - lam * (acc_ref[2 * head + 1, 0:dv, :] / acc_ref[2 * head + 1, dv:dv + 1, :]))
        ms = jnp.mean(o * o, axis=0, keepdims=True)
        heads.append(o * lax.rsqrt(ms + EPS) * subg_ref[head * dv:(head + 1) * dv, :] * (1.0 - lam_init))
    yc = jnp.concatenate(heads, axis=0).T.astype(BF16)

    y = (_dot(yabd_ref[:, 0:2 * gw], wo_ref[0:2 * gw, :])
         + _dot(yc, wo_ref[2 * gw:3 * gw, :])
         + _dot(yabd_ref[:, 2 * gw:3 * gw], wo_ref[3 * gw:4 * gw, :]))
    o_ref[...] = x_ref[...] + _rms(y, post_g_ref[...])


def _diff_attn(x, layer, b, s, qt, k, vt, yabd, wo, lamv, subg, post_g):
    n, d = x.shape
    gw = d // N_MIXERS
    dk = gw // DIFF_HEADS // 2
    tk = V7X_MXU_DIM
    tq = 2 * tk
    rc = 32
    nq = s // tq
    row = lambda bi, qi: (bi * nq + qi, 0)
    const = lambda bi, qi: (layer, 0, 0)
    n_hc = gw // dk
    est = (2 * s * (gw + vt.shape[1]) * 2 + 2 * d * d * 2 + 4 * tq * d * 4 + n_hc * gw * tq * 2
           + n_hc * tk * tq * (4 + 4 + 2) + 4 * tq * d * 4)
    kern = functools.partial(_diff_attn_kernel, dk=dk, tk=tk, rc=rc)
    return pl.pallas_call(
        kern,
        grid=(b, nq),
        in_specs=[
            pl.BlockSpec((None, gw, tq), lambda bi, qi: (bi, 0, qi)),
            pl.BlockSpec((None, s, gw), lambda bi, qi: (bi, 0, 0)),
            pl.BlockSpec((None, vt.shape[1], s), lambda bi, qi: (bi, 0, 0)),
            pl.BlockSpec((tq, 3 * gw), row),
            pl.BlockSpec((tq, d), row),
            pl.BlockSpec((None, d, d), const),
            pl.BlockSpec((None,) + lamv.shape[1:], const),
            pl.BlockSpec((None, gw, 1), const),
            pl.BlockSpec((None, 1, d), const),
        ],
        out_specs=pl.BlockSpec((tq, d), row),
        out_shape=jax.ShapeDtypeStruct((n, d), F32),
        scratch_shapes=[pltpu.VMEM((n_hc, gw, tq), BF16),
                        pltpu.VMEM((n_hc, tk, tq), F32),
                        pltpu.VMEM((n_hc, tk, tq), F32),
                        pltpu.VMEM((n_hc * V7X_SUBLANES, tq), F32),
                        pltpu.VMEM((n_hc * V7X_SUBLANES, tq), F32),
                        pltpu.VMEM((n_hc, tk, tq), BF16),
                        pltpu.VMEM((n_hc, tq), F32),
                        pltpu.VMEM((n_hc, 2 * dk + ONES_ROWS, tq), F32)],
        compiler_params=pltpu.CompilerParams(
            dimension_semantics=("arbitrary", "arbitrary"), vmem_limit_bytes=_vmem_limit(est)),
        name="diff_attn",
    )(qt, k, vt, yabd, x, wo, lamv, subg, post_g)


def _mem_proj_kernel(mem_ref, g_ref, wk_ref, wv_ref, kt_ref, v_ref):
    mn = _rms(mem_ref[...], g_ref[...]).astype(BF16)
    kt_ref[...] = _dot(mn, wk_ref[...]).T.astype(BF16)
    v_ref[...] = _dot(mn, wv_ref[...]).astype(BF16)


def _mem_proj(mem, layer, g, wk, wv):
    b, m, d = mem.shape
    const = lambda bi: (layer, 0, 0)
    est = 2 * 2 * d * d * 2 + 8 * m * d * 4
    return pl.pallas_call(
        _mem_proj_kernel,
        grid=(b,),
        in_specs=[
            pl.BlockSpec((None, m, d), lambda bi: (bi, 0, 0)),
            pl.BlockSpec((None, 1, d), const),
            pl.BlockSpec((None, d, d), const),
            pl.BlockSpec((None, d, d), const),
        ],
        out_specs=[
            pl.BlockSpec((None, d, m), lambda bi: (bi, 0, 0)),
            pl.BlockSpec((None, m, d), lambda bi: (bi, 0, 0)),
        ],
        out_shape=[jax.ShapeDtypeStruct((b, d, m), BF16), jax.ShapeDtypeStruct((b, m, d), BF16)],
        compiler_params=pltpu.CompilerParams(
            dimension_semantics=("arbitrary",), vmem_limit_bytes=_vmem_limit(est)),
        name="mem_proj",
    )(mem, g, wk, wv)


def _xattn_kernel(x_ref, pre_g_ref, wq_ref, kt_ref, v_ref, wo_ref, post_g_ref, o_ref,
                  q_ref, ox_ref):
    x = x_ref[...]
    tm, d = x.shape
    hd = d // X_HEADS
    h = _rms(x, pre_g_ref[...]).astype(BF16)
    for head in range(X_HEADS):
        cols = slice(head * hd, (head + 1) * hd)
        q_ref[:, cols] = _dot(h, wq_ref[:, cols]).astype(BF16)
    for head in range(X_HEADS):
        cols = slice(head * hd, (head + 1) * hd)
        for r0 in range(0, tm, XATTN_ROWS):
            rows = slice(r0, r0 + XATTN_ROWS)
            sc = _dot(q_ref[rows, cols], kt_ref[cols, :]) * (hd ** -0.5)
            p = jnp.exp(sc - jnp.max(sc, axis=-1, keepdims=True))
            l = jnp.sum(p, axis=-1, keepdims=True)
            ox_ref[rows, cols] = (_dot(p.astype(BF16), v_ref[:, cols]) / l).astype(BF16)
    o_ref[...] = x + _rms(_dot(ox_ref[...], wo_ref[...]), post_g_ref[...])


def _xattn(x, layer, s, tm, pre_g, wq, kt, v, wo, post_g):
    n, d = x.shape
    m = v.shape[1]
    nt = s // tm
    const = lambda i: (layer, 0, 0)
    est = 2 * 2 * d * d * 2 + 4 * d * m * 2 + 8 * tm * d * 4
    return pl.pallas_call(
        _xattn_kernel,
        grid=(n // tm,),
        in_specs=[
            pl.BlockSpec((tm, d), lambda i: (i, 0)),
            pl.BlockSpec((None, 1, d), const),
            pl.BlockSpec((None, d, d), const),
            pl.BlockSpec((None, d, m), lambda i: (i // nt, 0, 0)),
            pl.BlockSpec((None, m, d), lambda i: (i // nt, 0, 0)),
            pl.BlockSpec((None, d, d), const),
            pl.BlockSpec((None, 1, d), const),
        ],
        out_specs=pl.BlockSpec((tm, d), lambda i: (i, 0)),
        out_shape=jax.ShapeDtypeStruct((n, d), F32),
        scratch_shapes=[pltpu.VMEM((tm, d), BF16), pltpu.VMEM((tm, d), BF16)],
        compiler_params=pltpu.CompilerParams(
            dimension_semantics=("arbitrary",), vmem_limit_bytes=_vmem_limit(est)),
        name="xattn",
    )(x, pre_g, wq, kt, v, wo, post_g)


def _rope_tables(positions, dk, rot):
    inv_freq = ROPE_THETA ** (-jnp.arange(0, rot, 2, dtype=F32) / rot)
    ang = positions.astype(F32)[..., None] * inv_freq
    cos, sin = jnp.cos(ang), jnp.sin(ang)
    pad = cos.shape[:-1] + (dk - rot,)
    c = jnp.concatenate([cos, cos, jnp.ones(pad, F32)], axis=-1)
    sn = jnp.concatenate([-sin, sin, jnp.zeros(pad, F32)], axis=-1)
    reps = V7X_LANES // dk
    n = positions.size
    return (jnp.tile(c, (1, 1, reps)).reshape(n, V7X_LANES),
            jnp.tile(sn, (1, 1, reps)).reshape(n, V7X_LANES))


def kernel(x, mem, positions, f1_pre_g, f1_w_gate, f1_w_up, f1_w_down, f1_post_g, mix_pre_g, w_in, pool_w, pool_scale, sconv_w, lam_q1, lam_k1, lam_q2, lam_k2, diff_subln_g, conf_dw_w, conf_dw_b, conf_ln_g, conf_ln_b, w_o, mix_post_g, x_pre_g, x_mem_g, x_wq, x_wk, x_wv, x_wo, x_post_g, f2_pre_g, f2_w_gate, f2_w_up, f2_w_down, f2_post_g):
    b, s, d = x.shape
    depth = w_in.shape[0]
    gw = d // N_MIXERS
    dk = gw // DIFF_HEADS // 2
    rot = dk // 4
    tm = _row_tile(s)
    assert s % (2 * V7X_MXU_DIM) == 0 and V7X_MXU_DIM % CHUNK == 0 and tm >= HALO
    assert HALO >= conf_dw_w.shape[1] - 1 and HALO >= max(POOL_WINDOWS) - 1
    assert tm % MIX_ROWS == 0 and MIX_ROWS % CONV_ROWS == 0 and tm % XATTN_ROWS == 0

    bf = lambda w: w.astype(BF16)
    row = lambda g: g[:, None, :]
    ct, st = _rope_tables(positions, dk, rot)
    n_pg = pool_w.shape[1]
    poolw = bf(jnp.einsum("lgcd,gh->lgchd", pool_w, jnp.eye(n_pg, dtype=F32)).reshape(depth, gw, gw))
    lam_init = jnp.asarray([0.8 - 0.6 * math.exp(-0.3 * l) for l in range(depth)], F32)
    lamv = jnp.concatenate(
        [jnp.stack([lam_q1, lam_k1, lam_q2, lam_k2], axis=1).astype(F32),
         jnp.broadcast_to(lam_init[:, None, None], (depth, 1, dk)),
         jnp.zeros((depth, 3, dk), F32)], axis=1)
    subg = jnp.tile(diff_subln_g, (1, DIFF_HEADS))[:, :, None]
    f1 = (row(f1_pre_g), bf(f1_w_gate), bf(f1_w_up), bf(f1_w_down), row(f1_post_g))
    f2 = (row(f2_pre_g), bf(f2_w_gate), bf(f2_w_up), bf(f2_w_down), row(f2_post_g))
    w_in_b, w_o_b = bf(w_in), bf(w_o)
    wq_b, wk_b, wv_b, wo_b = bf(x_wq), bf(x_wk), bf(x_wv), bf(x_wo)

    xf = x.reshape(b * s, d)
    for l in range(depth):
        xf = _ffn(xf, l, *f1, tm)
        yabd, qt, k, vt = _mix_in(xf, l, b, s, tm, row(mix_pre_g), w_in_b, ct, st, poolw,
                                  row(pool_scale), sconv_w, conf_dw_w, row(conf_dw_b),
                                  row(conf_ln_g), row(conf_ln_b))
        xf = _diff_attn(xf, l, b, s, qt, k, vt, yabd, w_o_b, lamv, subg, row(mix_post_g))
        kt, v = _mem_proj(mem, l, row(x_mem_g), wk_b, wv_b)
        xf = _xattn(xf, l, s, tm, row(x_pre_g), wq_b, kt, v, wo_b, row(x_post_g))
        xf = _ffn(xf, l, *f2, tm)
    return xf.reshape(b, s, d)
```

```python
import functools
import math

import jax
import jax.numpy as jnp
from jax import lax
from jax.experimental import pallas as pl
from jax.experimental.pallas import tpu as pltpu

F32 = jnp.float32
BF16 = jnp.bfloat16

EPS = 1e-6
CHUNK = 64
N_MIXERS = 4
POOL_WINDOWS = (2, 4, 8, 16)
DIFF_HEADS = 4
ROPE_THETA = 500000.0
X_HEADS = 4

V7X_LANES = 128
V7X_SUBLANES = 8
V7X_MXU_DIM = 256
V7X_VMEM_BYTES = 64 * 1024 * 1024

V7X_BF16_SUBLANES = 16

HALO = 32
ONES_ROWS = V7X_BF16_SUBLANES
MIX_ROWS = 512
CONV_ROWS = 64
XATTN_ROWS = 128
NEG_BIG = -1e30
LOG2E = math.log2(math.e)


def _vmem_limit(estimate_bytes):
    return int(min(estimate_bytes + (12 << 20), V7X_VMEM_BYTES - (6 << 20)))


def _rms(x, g):
    return x * lax.rsqrt(jnp.mean(x * x, axis=-1, keepdims=True) + EPS) * g


def _dot(a, b):
    return jnp.dot(a, b, preferred_element_type=F32)


def _row_tile(rows_per_seq):
    for cand in (512, 256, 128):
        if rows_per_seq % cand == 0:
            return cand
    raise ValueError(f"sequence length {rows_per_seq} must be a multiple of 128")


def _ffn_kernel(x_ref, pre_g_ref, wg_ref, wu_ref, wd_ref, post_g_ref, o_ref):
    x = x_ref[...]
    h = _rms(x, pre_g_ref[...])
    g = _dot(h, wg_ref[...])
    u = _dot(h, wu_ref[...])
    y = _dot(g * jax.nn.sigmoid(g) * u, wd_ref[...])
    o_ref[...] = x + 0.5 * _rms(y, post_g_ref[...])


def _ffn(x, layer, pre_g, wg, wu, wd, post_g, tm):
    n, d = x.shape
    f = wg.shape[-1]
    const = lambda i: (layer, 0, 0)
    est = 3 * d * f * 4 + 4 * tm * d * 4 + 2 * tm * f * 4
    return pl.pallas_call(
        _ffn_kernel,
        grid=(n // tm,),
        in_specs=[
            pl.BlockSpec((tm, d), lambda i: (i, 0)),
            pl.BlockSpec((None, 1, d), const),
            pl.BlockSpec((None, d, f), const),
            pl.BlockSpec((None, d, f), const),
            pl.BlockSpec((None, f, d), const),
            pl.BlockSpec((None, 1, d), const),
        ],
        out_specs=pl.BlockSpec((tm, d), lambda i: (i, 0)),
        out_shape=jax.ShapeDtypeStruct((n, d), F32),
        compiler_params=pltpu.CompilerParams(
            dimension_semantics=("arbitrary",), vmem_limit_bytes=_vmem_limit(est)),
        name="ffn",
    )(x, pre_g, wg, wu, wd, post_g)


def _mix_in_kernel(x_ref, g_ref, w_in_ref, ct_ref, st_ref, poolw_ref, pscale_ref,
                   sconv_ref, confw_ref, confb_ref, lng_ref, lnb_ref,
                   yabd_ref, qt_ref, k_ref, vt_ref, ext_ref, dly_ref, *, gw, dk, rot):
    tm = x_ref.shape[0]
    t_idx = pl.program_id(1)
    sub = V7X_SUBLANES
    dv = 2 * dk
    pool_ch = gw // len(POOL_WINDOWS)
    n_s = sconv_ref.shape[0]
    n_c = confw_ref.shape[0]
    reps = gw // V7X_LANES

    @pl.when(t_idx == 0)
    def _():
        ext_ref[0:HALO, :] = jnp.zeros((HALO, ext_ref.shape[1]), F32)

    for s0 in range(0, tm, MIX_ROWS):
        rows = slice(s0, s0 + MIX_ROWS)
        erows = slice(HALO + s0, HALO + s0 + MIX_ROWS)
        h = _rms(x_ref[rows, :], g_ref[...])

        def proj(c):
            return _dot(h, w_in_ref[:, c * gw:(c + 1) * gw])

        def hist(col, shift):
            lo = HALO + s0 - shift
            return ext_ref[lo:lo + MIX_ROWS, col * gw:(col + 1) * gw]

        ext_ref[erows, 0:gw] = proj(0)
        frame = lax.broadcasted_iota(jnp.int32, (MIX_ROWS, V7X_LANES), 0) + (t_idx * tm + s0)
        lane = lax.broadcasted_iota(jnp.int32, (MIX_ROWS, V7X_LANES), 1)
        halves = []
        for hv in range(reps):
            groups = [g for g in range(len(POOL_WINDOWS)) if (g * pool_ch) // V7X_LANES == hv]
            wins = [POOL_WINDOWS[g] for g in groups]
            lo = hv * V7X_LANES
            cur = ext_ref[erows, lo:lo + V7X_LANES]
            run = cur
            prefix = {1: run}
            for j in range(1, max(wins)):
                run = run + ext_ref[HALO + s0 - j:HALO + s0 - j + MIX_ROWS, lo:lo + V7X_LANES]
                prefix[j + 1] = run
            csum = prefix[wins[-1]]
            win = jnp.full((MIX_ROWS, V7X_LANES), wins[-1], jnp.int32)
            for g, w in zip(groups[:-1][::-1], wins[:-1][::-1]):
                sel = lane < (g + 1) * pool_ch - lo
                csum = jnp.where(sel, prefix[w], csum)
                win = jnp.where(sel, w, win)
            cnt = jnp.minimum(frame + 1, win).astype(F32)
            halves.append(csum / cnt - cur)
        mixed = jnp.concatenate(halves, axis=1).astype(BF16)
        yabd_ref[rows, 0:gw] = (_dot(mixed, poolw_ref[...]) * pscale_ref[...]).astype(BF16)

        xs = proj(1)
        bg = proj(2)
        cg = proj(3)
        ext_ref[erows, gw:2 * gw] = cg * xs
        conv = sconv_ref[n_s - 1:n_s, :] * hist(1, 0)
        for j in range(n_s - 1):
            conv = conv + sconv_ref[j:j + 1, :] * hist(1, n_s - 1 - j)
        yabd_ref[rows, gw:2 * gw] = (bg * conv).astype(BF16)

        ga = proj(7)
        gg = proj(8)
        ext_ref[erows, 2 * gw:3 * gw] = ga * jax.nn.sigmoid(gg)
        j0 = 0 if s0 == 0 else s0 + HALO - sub
        j1 = s0 + MIX_ROWS + HALO - sub
        for r in range(1, sub):
            dly_ref[r - 1, j0:j1, :] = ext_ref[sub - r + j0:sub - r + j1, 2 * gw:3 * gw]
        for c0 in range(s0, s0 + MIX_ROWS, CONV_ROWS):
            u = confb_ref[...]
            for j in range(n_c):
                a, r = divmod(n_c - 1 - j, sub)
                if r == 0:
                    lo = HALO - sub * a + c0
                    src = ext_ref[lo:lo + CONV_ROWS, 2 * gw:3 * gw]
                else:
                    lo = HALO - sub - sub * a + c0
                    src = dly_ref[r - 1, lo:lo + CONV_ROWS, :]
                u = u + confw_ref[j:j + 1, :] * src
            mu = jnp.mean(u, axis=-1, keepdims=True)
            uc = u - mu
            ln = (uc * lax.rsqrt(jnp.mean(uc * uc, axis=-1, keepdims=True) + EPS) * lng_ref[...]
                  + lnb_ref[...])
            yabd_ref[c0:c0 + CONV_ROWS, 2 * gw:3 * gw] = (ln * jax.nn.sigmoid(ln)).astype(BF16)

        lane_q = lax.broadcasted_iota(jnp.int32, (MIX_ROWS, gw), 1)
        first_half = (lane_q % dk) < (rot // 2)
        cos_t = jnp.concatenate([ct_ref[rows, :]] * reps, axis=1)
        sin_t = jnp.concatenate([st_ref[rows, :]] * reps, axis=1)

        def rope(v):
            partner = jnp.where(first_half,
                                pltpu.roll(v, gw - rot // 2, axis=1),
                                pltpu.roll(v, rot // 2, axis=1))
            return v * cos_t + partner * sin_t

        qt_ref[:, rows] = (rope(proj(4)) * (dk ** -0.5 * LOG2E)).T.astype(BF16)
        k_ref[rows, :] = rope(proj(5)).astype(BF16)
        vt = proj(6).T.astype(BF16)
        for head in range(gw // dv):
            base = head * (dv + ONES_ROWS)
            vt_ref[base:base + dv, rows] = vt[head * dv:(head + 1) * dv, :]
            vt_ref[base + dv:base + dv + ONES_ROWS, rows] = jnp.ones((ONES_ROWS, MIX_ROWS), BF16)

    ext_ref[0:HALO, :] = ext_ref[tm:tm + HALO, :]


def _mix_in(x, layer, b, s, tm, pre_g, w_in, ct, st, poolw, pscale, sconv_w, conf_w,
            conf_b, ln_g, ln_b):
    n, d = x.shape
    gw = d // N_MIXERS
    in_cols = w_in.shape[-1]
    dk = gw // DIFF_HEADS // 2
    rot = dk // 4
    nt = s // tm
    vt_rows = DIFF_HEADS * (2 * dk + ONES_ROWS)
    row = lambda bi, ti: (bi * nt + ti, 0)
    const = lambda bi, ti: (layer, 0, 0)
    est = (2 * d * in_cols * 4 + 4 * tm * d * 4 + (HALO + tm) * (3 + V7X_SUBLANES - 1) * gw * 4
           + 12 * tm * gw * 4)
    kern = functools.partial(_mix_in_kernel, gw=gw, dk=dk, rot=rot)
    return pl.pallas_call(
        kern,
        grid=(b, nt),
        in_specs=[
            pl.BlockSpec((tm, d), row),
            pl.BlockSpec((None, 1, d), const),
            pl.BlockSpec((None, d, in_cols), const),
            pl.BlockSpec((tm, V7X_LANES), row),
            pl.BlockSpec((tm, V7X_LANES), row),
            pl.BlockSpec((None, gw, gw), const),
            pl.BlockSpec((None, 1, gw), const),
            pl.BlockSpec((None,) + sconv_w.shape[1:], const),
            pl.BlockSpec((None,) + conf_w.shape[1:], const),
            pl.BlockSpec((None, 1, gw), const),
            pl.BlockSpec((None, 1, gw), const),
            pl.BlockSpec((None, 1, gw), const),
        ],
        out_specs=[
            pl.BlockSpec((tm, 3 * gw), row),
            pl.BlockSpec((None, gw, tm), lambda bi, ti: (bi, 0, ti)),
            pl.BlockSpec((None, tm, gw), lambda bi, ti: (bi, ti, 0)),
            pl.BlockSpec((None, vt_rows, tm), lambda bi, ti: (bi, 0, ti)),
        ],
        out_shape=[
            jax.ShapeDtypeStruct((n, 3 * gw), BF16),
            jax.ShapeDtypeStruct((b, gw, s), BF16),
            jax.ShapeDtypeStruct((b, s, gw), BF16),
            jax.ShapeDtypeStruct((b, vt_rows, s), BF16),
        ],
        scratch_shapes=[pltpu.VMEM((HALO + tm, 3 * gw), F32),
                        pltpu.VMEM((V7X_SUBLANES - 1, HALO - V7X_SUBLANES + tm, gw), F32)],
        compiler_params=pltpu.CompilerParams(
            dimension_semantics=("arbitrary", "arbitrary"), vmem_limit_bytes=_vmem_limit(est)),
        name="mix_in",
    )(x, pre_g, w_in, ct, st, poolw, pscale, sconv_w, conf_w, conf_b, ln_g, ln_b)


def _diff_attn_kernel(qt_ref, k_ref, vt_ref, yabd_ref, x_ref, wo_ref, lam_ref, subg_ref,
                      post_g_ref, o_ref, qs_ref, sa_ref, sb_ref, cma_ref, cmb_ref, p_ref, m_ref,
                      acc_ref, *, dk, tk, rc):
    gw, tq = qt_ref.shape
    dv = 2 * dk
    dva = dv + ONES_ROWS
    n_hc = gw // dk
    n_rc = tk // rc
    kpq = tq // tk
    qi = pl.program_id(1)

    for hc in range(n_hc):
        qs_ref[hc] = jnp.zeros((gw, tq), BF16)
        qs_ref[hc, hc * dk:(hc + 1) * dk, :] = qt_ref[hc * dk:(hc + 1) * dk, :]
    acc_ref[...] = jnp.zeros(acc_ref.shape, F32)
    m_ref[...] = jnp.full(m_ref.shape, NEG_BIG, F32)

    def fold8(v, op):
        return op(v.reshape(v.shape[0] // V7X_SUBLANES, V7X_SUBLANES, v.shape[1]), axis=0)

    def rows8(hc):
        return slice(hc * V7X_SUBLANES, (hc + 1) * V7X_SUBLANES)

    def scores(blk, s_ref, cm_ref, q0=0):
        kb = k_ref[pl.ds(pl.multiple_of(blk * tk, tk), tk), :]
        for hc in range(n_hc):
            sc = _dot(kb, qs_ref[hc, :, q0:])
            s_ref[hc, :, q0:] = sc
            if cm_ref is not None:
                cm_ref[rows8(hc), q0:] = fold8(sc, jnp.max)

    def softmax_pv(blk, s_ref, cm_ref, diag=None, q0=0):
        off = pl.multiple_of(blk * tk, tk)
        masked = diag is not None
        qchunk = (lax.broadcasted_iota(jnp.int32, (1, tq - q0), 1) + q0) // CHUNK

        def chunk(hc, r):
            sv = s_ref[hc, r * rc:(r + 1) * rc, q0:]
            if masked:
                kc = (diag * tk + r * rc) // CHUNK
                if kc > q0 // CHUNK:
                    sv = jnp.where(qchunk >= kc, sv, NEG_BIG)
            return sv

        for hc in range(n_hc):
            head = hc // 2
            if masked:
                cm = fold8(chunk(hc, 0), jnp.max)
                for r in range(1, n_rc):
                    cm = jnp.maximum(cm, fold8(chunk(hc, r), jnp.max))
            else:
                cm = cm_ref[rows8(hc), q0:]
            m_old = m_ref[hc:hc + 1, q0:]
            m_new = jnp.maximum(m_old, jnp.max(cm, axis=0, keepdims=True))
            alpha = jnp.exp2(m_old - m_new)
            m_ref[hc:hc + 1, q0:] = m_new
            for r in range(n_rc):
                p_ref[hc, r * rc:(r + 1) * rc, q0:] = jnp.exp2(chunk(hc, r) - m_new).astype(BF16)
            vt = vt_ref[head * dva:(head + 1) * dva, pl.ds(off, tk)]
            acc_ref[hc, :, q0:] = alpha * acc_ref[hc, :, q0:] + _dot(vt, p_ref[hc, :, q0:])

    assert kpq == 2
    scores(2 * qi, sa_ref, None)
    scores(2 * qi + 1, sb_ref, None, q0=tk)
    softmax_pv(2 * qi, sa_ref, None, diag=0)
    softmax_pv(2 * qi + 1, sb_ref, None, diag=1, q0=tk)

    def pair(pi, carry):
        scores(2 * pi, sa_ref, cma_ref)
        scores(2 * pi + 1, sb_ref, cmb_ref)
        softmax_pv(2 * pi, sa_ref, cma_ref)
        softmax_pv(2 * pi + 1, sb_ref, cmb_ref)
        return carry

    lax.fori_loop(0, qi, pair, 0)

    lam_init = lam_ref[4:5, 0:1]
    lam = (jnp.exp(jnp.sum(lam_ref[0:1, :] * lam_ref[1:2, :], axis=1, keepdims=True))
           - jnp.exp(jnp.sum(lam_ref[2:3, :] * lam_ref[3:4, :], axis=1, keepdims=True))
           + lam_init)
    heads = []
    for head in range(n_hc // 2):
        o = (acc_ref[2 * head, 0:dv, :] / acc_ref[2 * head, dv:dv + 1, :]
             - lam * (acc_ref[2 * head + 1, 0:dv, :] / acc_ref[2 * head + 1, dv:dv + 1, :]))
        ms = jnp.mean(o * o, axis=0, keepdims=True)
        heads.append(o * lax.rsqrt(ms + EPS) * subg_ref[head * dv:(head + 1) * dv, :] * (1.0 - lam_init))
    yc = jnp.concatenate(heads, axis=0).T.astype(BF16)

    y = (_dot(yabd_ref[:, 0:2 * gw], wo_ref[0:2 * gw, :])
         + _dot(yc, wo_ref[2 * gw:3 * gw, :])
         + _dot(yabd_ref[:, 2 * gw:3 * gw], wo_ref[3 * gw:4 * gw, :]))
    o_ref[...] = x_ref[...] + _rms(y, post_g_ref[...])


def _diff_attn(x, layer, b, s, qt, k, vt, yabd, wo, lamv, subg, post_g):
    n, d = x.shape
    gw = d // N_MIXERS
    dk = gw // DIFF_HEADS // 2
    tk = V7X_MXU_DIM
    tq = 2 * tk
    rc = 32
    nq = s // tq
    row = lambda bi, qi: (bi * nq + qi, 0)
    const = lambda bi, qi: (layer, 0, 0)
    n_hc = gw // dk
    est = (2 * s * (gw + vt.shape[1]) * 2 + 2 * d * d * 2 + 4 * tq * d * 4 + n_hc * gw * tq * 2
           + n_hc * tk * tq * (4 + 4 + 2) + 4 * tq * d * 4)
    kern = functools.partial(_diff_attn_kernel, dk=dk, tk=tk, rc=rc)
    return pl.pallas_call(
        kern,
        grid=(b, nq),
        in_specs=[
            pl.BlockSpec((None, gw, tq), lambda bi, qi: (bi, 0, qi)),
            pl.BlockSpec((None, s, gw), lambda bi, qi: (bi, 0, 0)),
            pl.BlockSpec((None, vt.shape[1], s), lambda bi, qi: (bi, 0, 0)),
            pl.BlockSpec((tq, 3 * gw), row),
            pl.BlockSpec((tq, d), row),
            pl.BlockSpec((None, d, d), const),
            pl.BlockSpec((None,) + lamv.shape[1:], const),
            pl.BlockSpec((None, gw, 1), const),
            pl.BlockSpec((None, 1, d), const),
        ],
        out_specs=pl.BlockSpec((tq, d), row),
        out_shape=jax.ShapeDtypeStruct((n, d), F32),
        scratch_shapes=[pltpu.VMEM((n_hc, gw, tq), BF16),
                        pltpu.VMEM((n_hc, tk, tq), F32),
                        pltpu.VMEM((n_hc, tk, tq), F32),
                        pltpu.VMEM((n_hc * V7X_SUBLANES, tq), F32),
                        pltpu.VMEM((n_hc * V7X_SUBLANES, tq), F32),
                        pltpu.VMEM((n_hc, tk, tq), BF16),
                        pltpu.VMEM((n_hc, tq), F32),
                        pltpu.VMEM((n_hc, 2 * dk + ONES_ROWS, tq), F32)],
        compiler_params=pltpu.CompilerParams(
            dimension_semantics=("arbitrary", "arbitrary"), vmem_limit_bytes=_vmem_limit(est)),
        name="diff_attn",
    )(qt, k, vt, yabd, x, wo, lamv, subg, post_g)


def _mem_proj_kernel(mem_ref, g_ref, wk_ref, wv_ref, kt_ref, v_ref):
    mn = _rms(mem_ref[...], g_ref[...])
    kt_ref[...] = _dot(mn, wk_ref[...]).T.astype(BF16)
    v_ref[...] = _dot(mn, wv_ref[...]).astype(BF16)


def _mem_proj(mem, layer, g, wk, wv):
    b, m, d = mem.shape
    const = lambda bi: (layer, 0, 0)
    est = 2 * 2 * d * d * 4 + 8 * m * d * 4
    return pl.pallas_call(
        _mem_proj_kernel,
        grid=(b,),
        in_specs=[
            pl.BlockSpec((None, m, d), lambda bi: (bi, 0, 0)),
            pl.BlockSpec((None, 1, d), const),
            pl.BlockSpec((None, d, d), const),
            pl.BlockSpec((None, d, d), const),
        ],
        out_specs=[
            pl.BlockSpec((None, d, m), lambda bi: (bi, 0, 0)),
            pl.BlockSpec((None, m, d), lambda bi: (bi, 0, 0)),
        ],
        out_shape=[jax.ShapeDtypeStruct((b, d, m), BF16), jax.ShapeDtypeStruct((b, m, d), BF16)],
        compiler_params=pltpu.CompilerParams(
            dimension_semantics=("arbitrary",), vmem_limit_bytes=_vmem_limit(est)),
        name="mem_proj",
    )(mem, g, wk, wv)


def _xattn_kernel(x_ref, pre_g_ref, wq_ref, kt_ref, v_ref, wo_ref, post_g_ref, o_ref,
                  q_ref, ox_ref):
    x = x_ref[...]
    tm, d = x.shape
    hd = d // X_HEADS
    h = _rms(x, pre_g_ref[...])
    for head in range(X_HEADS):
        cols = slice(head * hd, (head + 1) * hd)
        q_ref[:, cols] = _dot(h, wq_ref[:, cols]).astype(BF16)
    for head in range(X_HEADS):
        cols = slice(head * hd, (head + 1) * hd)
        for r0 in range(0, tm, XATTN_ROWS):
            rows = slice(r0, r0 + XATTN_ROWS)
            sc = _dot(q_ref[rows, cols], kt_ref[cols, :]) * (hd ** -0.5)
            p = jnp.exp(sc - jnp.max(sc, axis=-1, keepdims=True))
            l = jnp.sum(p, axis=-1, keepdims=True)
            ox_ref[rows, cols] = (_dot(p.astype(BF16), v_ref[:, cols]) / l).astype(BF16)
    o_ref[...] = x + _rms(_dot(ox_ref[...], wo_ref[...]), post_g_ref[...])


def _xattn(x, layer, s, tm, pre_g, wq, kt, v, wo, post_g):
    n, d = x.shape
    m = v.shape[1]
    nt = s // tm
    const = lambda i: (layer, 0, 0)
    est = 2 * d * d * (4 + 2) + 4 * d * m * 2 + 8 * tm * d * 4
    return pl.pallas_call(
        _xattn_kernel,
        grid=(n // tm,),
        in_specs=[
            pl.BlockSpec((tm, d), lambda i: (i, 0)),
            pl.BlockSpec((None, 1, d), const),
            pl.BlockSpec((None, d, d), const),
            pl.BlockSpec((None, d, m), lambda i: (i // nt, 0, 0)),
            pl.BlockSpec((None, m, d), lambda i: (i // nt, 0, 0)),
            pl.BlockSpec((None, d, d), const),
            pl.BlockSpec((None, 1, d), const),
        ],
        out_specs=pl.BlockSpec((tm, d), lambda i: (i, 0)),
        out_shape=jax.ShapeDtypeStruct((n, d), F32),
        scratch_shapes=[pltpu.VMEM((tm, d), BF16), pltpu.VMEM((tm, d), BF16)],
        compiler_params=pltpu.CompilerParams(
            dimension_semantics=("arbitrary",), vmem_limit_bytes=_vmem_limit(est)),
        name="xattn",
    )(x, pre_g, wq, kt, v, wo, post_g)


def _rope_tables(positions, dk, rot):
    inv_freq = ROPE_THETA ** (-jnp.arange(0, rot, 2, dtype=F32) / rot)
    ang = positions.astype(F32)[..., None] * inv_freq
    cos, sin = jnp.cos(ang), jnp.sin(ang)
    pad = cos.shape[:-1] + (dk - rot,)
    c = jnp.concatenate([cos, cos, jnp.ones(pad, F32)], axis=-1)
    sn = jnp.concatenate([-sin, sin, jnp.zeros(pad, F32)], axis=-1)
    reps = V7X_LANES // dk
    n = positions.size
    return (jnp.tile(c, (1, 1, reps)).reshape(n, V7X_LANES),
            jnp.tile(sn, (1, 1, reps)).reshape(n, V7X_LANES))


def kernel(x, mem, positions, f1_pre_g, f1_w_gate, f1_w_up, f1_w_down, f1_post_g, mix_pre_g, w_in, pool_w, pool_scale, sconv_w, lam_q1, lam_k1, lam_q2, lam_k2, diff_subln_g, conf_dw_w, conf_dw_b, conf_ln_g, conf_ln_b, w_o, mix_post_g, x_pre_g, x_mem_g, x_wq, x_wk, x_wv, x_wo, x_post_g, f2_pre_g, f2_w_gate, f2_w_up, f2_w_down, f2_post_g):
    b, s, d = x.shape
    depth = w_in.shape[0]
    gw = d // N_MIXERS
    dk = gw // DIFF_HEADS // 2
    rot = dk // 4
    tm = _row_tile(s)
    assert s % (2 * V7X_MXU_DIM) == 0 and V7X_MXU_DIM % CHUNK == 0 and tm >= HALO
    assert HALO >= conf_dw_w.shape[1] - 1 and HALO >= max(POOL_WINDOWS) - 1
    assert tm % MIX_ROWS == 0 and MIX_ROWS % CONV_ROWS == 0 and tm % XATTN_ROWS == 0

    bf = lambda w: w.astype(BF16)
    row = lambda g: g[:, None, :]
    ct, st = _rope_tables(positions, dk, rot)
    n_pg = pool_w.shape[1]
    poolw = bf(jnp.einsum("lgcd,gh->lgchd", pool_w, jnp.eye(n_pg, dtype=F32)).reshape(depth, gw, gw))
    lam_init = jnp.asarray([0.8 - 0.6 * math.exp(-0.3 * l) for l in range(depth)], F32)
    lamv = jnp.concatenate(
        [jnp.stack([lam_q1, lam_k1, lam_q2, lam_k2], axis=1).astype(F32),
         jnp.broadcast_to(lam_init[:, None, None], (depth, 1, dk)),
         jnp.zeros((depth, 3, dk), F32)], axis=1)
    subg = jnp.tile(diff_subln_g, (1, DIFF_HEADS))[:, :, None]
    f1 = (row(f1_pre_g), f1_w_gate, f1_w_up, f1_w_down, row(f1_post_g))
    f2 = (row(f2_pre_g), f2_w_gate, f2_w_up, f2_w_down, row(f2_post_g))
    w_o_b, wo_b = bf(w_o), bf(x_wo)

    xf = x.reshape(b * s, d)
    for l in range(depth):
        xf = _ffn(xf, l, *f1, tm)
        yabd, qt, k, vt = _mix_in(xf, l, b, s, tm, row(mix_pre_g), w_in, ct, st, poolw,
                                  row(pool_scale), sconv_w, conf_dw_w, row(conf_dw_b),
                                  row(conf_ln_g), row(conf_ln_b))
        xf = _diff_attn(xf, l, b, s, qt, k, vt, yabd, w_o_b, lamv, subg, row(mix_post_g))
        kt, v = _mem_proj(mem, l, row(x_mem_g), x_wk, x_wv)
        xf = _xattn(xf, l, s, tm, row(x_pre_g), x_wq, kt, v, wo_b, row(x_post_g))
        xf = _ffn(xf, l, *f2, tm)
    return xf.reshape(b, s, d)
```

```python
import functools
import math

import jax
import jax.numpy as jnp
from jax import lax
from jax.experimental import pallas as pl
from jax.experimental.pallas import tpu as pltpu

F32 = jnp.float32
BF16 = jnp.bfloat16

EPS = 1e-6
CHUNK = 64
N_MIXERS = 4
POOL_WINDOWS = (2, 4, 8, 16)
DIFF_HEADS = 4
ROPE_THETA = 500000.0
X_HEADS = 4

V7X_LANES = 128
V7X_SUBLANES = 8
V7X_MXU_DIM = 256
V7X_VMEM_BYTES = 64 * 1024 * 1024

V7X_BF16_SUBLANES = 16

HALO = 32
ONES_ROWS = V7X_BF16_SUBLANES
CONV_ROWS = 64
XATTN_ROWS = 128
NEG_BIG = -1e30
LOG2E = math.log2(math.e)


def _vmem_limit(estimate_bytes):
    return int(min(estimate_bytes + (12 << 20), V7X_VMEM_BYTES - (6 << 20)))


def _rms(x, g):
    return x * lax.rsqrt(jnp.mean(x * x, axis=-1, keepdims=True) + EPS) * g


def _dot(a, b):
    return jnp.dot(a, b, preferred_element_type=F32)


def _row_tile(rows_per_seq):
    for cand in (512, 256, 128):
        if rows_per_seq % cand == 0:
            return cand
    raise ValueError(f"sequence length {rows_per_seq} must be a multiple of 128")


def _ffn_kernel(x_ref, pre_g_ref, wg_ref, wu_ref, wd_ref, post_g_ref, o_ref):
    x = x_ref[...]
    h = _rms(x, pre_g_ref[...])
    g = _dot(h, wg_ref[...])
    u = _dot(h, wu_ref[...])
    y = _dot(g * jax.nn.sigmoid(g) * u, wd_ref[...])
    o_ref[...] = x + 0.5 * _rms(y, post_g_ref[...])


def _ffn(x, layer, pre_g, wg, wu, wd, post_g, tm):
    n, d = x.shape
    f = wg.shape[-1]
    const = lambda i: (layer, 0, 0)
    est = 3 * d * f * 4 + 4 * tm * d * 4 + 2 * tm * f * 4
    return pl.pallas_call(
        _ffn_kernel,
        grid=(n // tm,),
        in_specs=[
            pl.BlockSpec((tm, d), lambda i: (i, 0)),
            pl.BlockSpec((None, 1, d), const),
            pl.BlockSpec((None, d, f), const),
            pl.BlockSpec((None, d, f), const),
            pl.BlockSpec((None, f, d), const),
            pl.BlockSpec((None, 1, d), const),
        ],
        out_specs=pl.BlockSpec((tm, d), lambda i: (i, 0)),
        out_shape=jax.ShapeDtypeStruct((n, d), F32),
        compiler_params=pltpu.CompilerParams(
            dimension_semantics=("arbitrary",), vmem_limit_bytes=_vmem_limit(est)),
        name="ffn",
    )(x, pre_g, wg, wu, wd, post_g)


def _mix_in_kernel(x_ref, g_ref, w_in_ref, ct_ref, st_ref, poolw_ref, pscale_ref,
                   sconv_ref, confw_ref, confb_ref, lng_ref, lnb_ref,
                   yabd_ref, qt_ref, k_ref, vt_ref, ext_ref, dly_ref, *, gw, dk, rot):
    tm = x_ref.shape[0]
    t_idx = pl.program_id(1)
    sub = V7X_SUBLANES
    dv = 2 * dk
    pool_ch = gw // len(POOL_WINDOWS)
    n_s = sconv_ref.shape[0]
    n_c = confw_ref.shape[0]
    reps = gw // V7X_LANES
    erows = slice(HALO, HALO + tm)

    @pl.when(t_idx == 0)
    def _():
        ext_ref[0:HALO, :] = jnp.zeros((HALO, ext_ref.shape[1]), F32)

    h = _rms(x_ref[...], g_ref[...])

    def proj(c):
        return _dot(h, w_in_ref[:, c * gw:(c + 1) * gw])

    def hist(col, shift):
        return ext_ref[HALO - shift:HALO - shift + tm, col * gw:(col + 1) * gw]

    def pool_mixer():
        ext_ref[erows, 0:gw] = proj(0)
        frame = lax.broadcasted_iota(jnp.int32, (tm, V7X_LANES), 0) + t_idx * tm
        lane = lax.broadcasted_iota(jnp.int32, (tm, V7X_LANES), 1)
        halves = []
        for hv in range(reps):
            groups = [g for g in range(len(POOL_WINDOWS)) if (g * pool_ch) // V7X_LANES == hv]
            wins = [POOL_WINDOWS[g] for g in groups]
            lo = hv * V7X_LANES
            cur = ext_ref[erows, lo:lo + V7X_LANES]
            run = cur
            prefix = {1: run}
            for j in range(1, max(wins)):
                run = run + ext_ref[HALO - j:HALO - j + tm, lo:lo + V7X_LANES]
                prefix[j + 1] = run
            csum = prefix[wins[-1]]
            win = jnp.full((tm, V7X_LANES), wins[-1], jnp.int32)
            for g, w in zip(groups[:-1][::-1], wins[:-1][::-1]):
                sel = lane < (g + 1) * pool_ch - lo
                csum = jnp.where(sel, prefix[w], csum)
                win = jnp.where(sel, w, win)
            cnt = jnp.minimum(frame + 1, win).astype(F32)
            halves.append(csum / cnt - cur)
        mixed = jnp.concatenate(halves, axis=1).astype(BF16)
        yabd_ref[:, 0:gw] = (_dot(mixed, poolw_ref[...]) * pscale_ref[...]).astype(BF16)

    def sconv_mixer():
        xs = proj(1)
        bg = proj(2)
        cg = proj(3)
        ext_ref[erows, gw:2 * gw] = cg * xs
        conv = sconv_ref[n_s - 1:n_s, :] * hist(1, 0)
        for j in range(n_s - 1):
            conv = conv + sconv_ref[j:j + 1, :] * hist(1, n_s - 1 - j)
        yabd_ref[:, gw:2 * gw] = (bg * conv).astype(BF16)

    def rope(v):
        lane_q = lax.broadcasted_iota(jnp.int32, (tm, gw), 1)
        first_half = (lane_q % dk) < (rot // 2)
        cos_t = jnp.concatenate([ct_ref[...]] * reps, axis=1)
        sin_t = jnp.concatenate([st_ref[...]] * reps, axis=1)
        partner = jnp.where(first_half,
                            pltpu.roll(v, gw - rot // 2, axis=1),
                            pltpu.roll(v, rot // 2, axis=1))
        return v * cos_t + partner * sin_t

    def q_out():
        qt_ref[...] = (rope(proj(4)) * (dk ** -0.5 * LOG2E)).T.astype(BF16)

    def k_out():
        k_ref[...] = rope(proj(5)).astype(BF16)

    def v_out():
        vt = proj(6).T.astype(BF16)
        for head in range(gw // dv):
            base = head * (dv + ONES_ROWS)
            vt_ref[base:base + dv, :] = vt[head * dv:(head + 1) * dv, :]
            vt_ref[base + dv:base + dv + ONES_ROWS, :] = jnp.ones((ONES_ROWS, tm), BF16)

    ext_ref[erows, 2 * gw:3 * gw] = proj(7) * jax.nn.sigmoid(proj(8))
    for r in range(1, sub):
        dly_ref[r - 1] = ext_ref[sub - r:sub - r + dly_ref.shape[1], 2 * gw:3 * gw]
    others = [q_out, k_out, v_out, sconv_mixer, pool_mixer]
    for c0 in range(0, tm, CONV_ROWS):
        u = confb_ref[...]
        for j in range(n_c):
            a, r = divmod(n_c - 1 - j, sub)
            if r == 0:
                lo = HALO - sub * a + c0
                src = ext_ref[lo:lo + CONV_ROWS, 2 * gw:3 * gw]
            else:
                lo = HALO - sub - sub * a + c0
                src = dly_ref[r - 1, lo:lo + CONV_ROWS, :]
            u = u + confw_ref[j:j + 1, :] * src
        mu = jnp.mean(u, axis=-1, keepdims=True)
        uc = u - mu
        ln = (uc * lax.rsqrt(jnp.mean(uc * uc, axis=-1, keepdims=True) + EPS) * lng_ref[...]
              + lnb_ref[...])
        yabd_ref[c0:c0 + CONV_ROWS, 2 * gw:3 * gw] = (ln * jax.nn.sigmoid(ln)).astype(BF16)
        if others:
            others.pop(0)()
    for mixer in others:
        mixer()

    ext_ref[0:HALO, :] = ext_ref[tm:tm + HALO, :]


def _mix_in(x, layer, b, s, tm, pre_g, w_in, ct, st, poolw, pscale, sconv_w, conf_w,
            conf_b, ln_g, ln_b):
    n, d = x.shape
    gw = d // N_MIXERS
    in_cols = w_in.shape[-1]
    dk = gw // DIFF_HEADS // 2
    rot = dk // 4
    nt = s // tm
    vt_rows = DIFF_HEADS * (2 * dk + ONES_ROWS)
    row = lambda bi, ti: (bi * nt + ti, 0)
    const = lambda bi, ti: (layer, 0, 0)
    est = (2 * d * in_cols * 4 + 4 * tm * d * 4 + (HALO + tm) * (3 + V7X_SUBLANES - 1) * gw * 4
           + 12 * tm * gw * 4)
    kern = functools.partial(_mix_in_kernel, gw=gw, dk=dk, rot=rot)
    return pl.pallas_call(
        kern,
        grid=(b, nt),
        in_specs=[
            pl.BlockSpec((tm, d), row),
            pl.BlockSpec((None, 1, d), const),
            pl.BlockSpec((None, d, in_cols), const),
            pl.BlockSpec((tm, V7X_LANES), row),
            pl.BlockSpec((tm, V7X_LANES), row),
            pl.BlockSpec((None, gw, gw), const),
            pl.BlockSpec((None, 1, gw), const),
            pl.BlockSpec((None,) + sconv_w.shape[1:], const),
            pl.BlockSpec((None,) + conf_w.shape[1:], const),
            pl.BlockSpec((None, 1, gw), const),
            pl.BlockSpec((None, 1, gw), const),
            pl.BlockSpec((None, 1, gw), const),
        ],
        out_specs=[
            pl.BlockSpec((tm, 3 * gw), row),
            pl.BlockSpec((None, gw, tm), lambda bi, ti: (bi, 0, ti)),
            pl.BlockSpec((None, tm, gw), lambda bi, ti: (bi, ti, 0)),
            pl.BlockSpec((None, vt_rows, tm), lambda bi, ti: (bi, 0, ti)),
        ],
        out_shape=[
            jax.ShapeDtypeStruct((n, 3 * gw), BF16),
            jax.ShapeDtypeStruct((b, gw, s), BF16),
            jax.ShapeDtypeStruct((b, s, gw), BF16),
            jax.ShapeDtypeStruct((b, vt_rows, s), BF16),
        ],
        scratch_shapes=[pltpu.VMEM((HALO + tm, 3 * gw), F32),
                        pltpu.VMEM((V7X_SUBLANES - 1, HALO - V7X_SUBLANES + tm, gw), F32)],
        compiler_params=pltpu.CompilerParams(
            dimension_semantics=("arbitrary", "arbitrary"), vmem_limit_bytes=_vmem_limit(est)),
        name="mix_in",
    )(x, pre_g, w_in, ct, st, poolw, pscale, sconv_w, conf_w, conf_b, ln_g, ln_b)


def _diff_attn_kernel(qt_ref, k_ref, vt_ref, yabd_ref, x_ref, wo_ref, lam_ref, subg_ref,
                      post_g_ref, o_ref, qs_ref, sa_ref, sb_ref, cma_ref, cmb_ref, p_ref, m_ref,
                      acc_ref, *, dk, tk, rc):
    gw, tq = qt_ref.shape
    dv = 2 * dk
    dva = dv + ONES_ROWS
    n_hc = gw // dk
    n_rc = tk // rc
    kpq = tq // tk
    qi = pl.program_id(1)

    for hc in range(n_hc):
        qs_ref[hc] = jnp.zeros((gw, tq), BF16)
        qs_ref[hc, hc * dk:(hc + 1) * dk, :] = qt_ref[hc * dk:(hc + 1) * dk, :]
    acc_ref[...] = jnp.zeros(acc_ref.shape, F32)
    m_ref[...] = jnp.full(m_ref.shape, NEG_BIG, F32)

    def fold8(v, op):
        return op(v.reshape(v.shape[0] // V7X_SUBLANES, V7X_SUBLANES, v.shape[1]), axis=0)

    def rows8(hc):
        return slice(hc * V7X_SUBLANES, (hc + 1) * V7X_SUBLANES)

    def scores(blk, s_ref, cm_ref, q0=0):
        kb = k_ref[pl.ds(pl.multiple_of(blk * tk, tk), tk), :]
        for hc in range(n_hc):
            sc = _dot(kb, qs_ref[hc, :, q0:])
            s_ref[hc, :, q0:] = sc
            if cm_ref is not None:
                cm_ref[rows8(hc), q0:] = fold8(sc, jnp.max)

    def softmax_pv(blk, s_ref, cm_ref, diag=None, q0=0):
        off = pl.multiple_of(blk * tk, tk)
        masked = diag is not None
        qchunk = (lax.broadcasted_iota(jnp.int32, (1, tq - q0), 1) + q0) // CHUNK

        def chunk(hc, r):
            sv = s_ref[hc, r * rc:(r + 1) * rc, q0:]
            if masked:
                kc = (diag * tk + r * rc) // CHUNK
                if kc > q0 // CHUNK:
                    sv = jnp.where(qchunk >= kc, sv, NEG_BIG)
            return sv

        for hc in range(n_hc):
            head = hc // 2
            if masked:
                cm = fold8(chunk(hc, 0), jnp.max)
                for r in range(1, n_rc):
                    cm = jnp.maximum(cm, fold8(chunk(hc, r), jnp.max))
            else:
                cm = cm_ref[rows8(hc), q0:]
            m_old = m_ref[hc:hc + 1, q0:]
            m_new = jnp.maximum(m_old, jnp.max(cm, axis=0, keepdims=True))
            alpha = jnp.exp2(m_old - m_new)
            m_ref[hc:hc + 1, q0:] = m_new
            for r in range(n_rc):
                p_ref[hc, r * rc:(r + 1) * rc, q0:] = jnp.exp2(chunk(hc, r) - m_new).astype(BF16)
            vt = vt_ref[head * dva:(head + 1) * dva, pl.ds(off, tk)]
            acc_ref[hc, :, q0:] = alpha * acc_ref[hc, :, q0:] + _dot(vt, p_ref[hc, :, q0:])

    assert kpq == 2
    scores(2 * qi, sa_ref, None)
    scores(2 * qi + 1, sb_ref, None, q0=tk)
    softmax_pv(2 * qi, sa_ref, None, diag=0)
    softmax_pv(2 * qi + 1, sb_ref, None, diag=1, q0=tk)

    def pair(pi, carry):
        scores(2 * pi, sa_ref, cma_ref)
        scores(2 * pi + 1, sb_ref, cmb_ref)
        softmax_pv(2 * pi, sa_ref, cma_ref)
        softmax_pv(2 * pi + 1, sb_ref, cmb_ref)
        return carry

    lax.fori_loop(0, qi, pair, 0)

    lam_init = lam_ref[4:5, 0:1]
    lam = (jnp.exp(jnp.sum(lam_ref[0:1, :] * lam_ref[1:2, :], axis=1, keepdims=True))
           - jnp.exp(jnp.sum(lam_ref[2:3, :] * lam_ref[3:4, :], axis=1, keepdims=True))
           + lam_init)
    heads = []
    for head in range(n_hc // 2):
        o = (acc_ref[2 * head, 0:dv, :] / acc_ref[2 * head, dv:dv + 1, :]
             - lam * (acc_ref[2 * head + 1, 0:dv, :] / acc_ref[2 * head + 1, dv:dv + 1, :]))
        ms = jnp.mean(o * o, axis=0, keepdims=True)
        heads.append(o * lax.rsqrt(ms + EPS) * subg_ref[head * dv:(head + 1) * dv, :] * (1.0 - lam_init))
    yc = jnp.concatenate(heads, axis=0).T.astype(BF16)

    y = (_dot(yabd_ref[:, 0:2 * gw], wo_ref[0:2 * gw, :])
         + _dot(yc, wo_ref[2 * gw:3 * gw, :])
         + _dot(yabd_ref[:, 2 * gw:3 * gw], wo_ref[3 * gw:4 * gw, :]))
    o_ref[...] = x_ref[...] + _rms(y, post_g_ref[...])


def _diff_attn(x, layer, b, s, qt, k, vt, yabd, wo, lamv, subg, post_g):
    n, d = x.shape
    gw = d // N_MIXERS
    dk = gw // DIFF_HEADS // 2
    tk = V7X_MXU_DIM
    tq = 2 * tk
    rc = 32
    nq = s // tq
    row = lambda bi, qi: (bi * nq + qi, 0)
    const = lambda bi, qi: (layer, 0, 0)
    n_hc = gw // dk
    est = (2 * s * (gw + vt.shape[1]) * 2 + 2 * d * d * 2 + 4 * tq * d * 4 + n_hc * gw * tq * 2
           + n_hc * tk * tq * (4 + 4 + 2) + 4 * tq * d * 4)
    kern = functools.partial(_diff_attn_kernel, dk=dk, tk=tk, rc=rc)
    return pl.pallas_call(
        kern,
        grid=(b, nq),
        in_specs=[
            pl.BlockSpec((None, gw, tq), lambda bi, qi: (bi, 0, qi)),
            pl.BlockSpec((None, s, gw), lambda bi, qi: (bi, 0, 0)),
            pl.BlockSpec((None, vt.shape[1], s), lambda bi, qi: (bi, 0, 0)),
            pl.BlockSpec((tq, 3 * gw), row),
            pl.BlockSpec((tq, d), row),
            pl.BlockSpec((None, d, d), const),
            pl.BlockSpec((None,) + lamv.shape[1:], const),
            pl.BlockSpec((None, gw, 1), const),
            pl.BlockSpec((None, 1, d), const),
        ],
        out_specs=pl.BlockSpec((tq, d), row),
        out_shape=jax.ShapeDtypeStruct((n, d), F32),
        scratch_shapes=[pltpu.VMEM((n_hc, gw, tq), BF16),
                        pltpu.VMEM((n_hc, tk, tq), F32),
                        pltpu.VMEM((n_hc, tk, tq), F32),
                        pltpu.VMEM((n_hc * V7X_SUBLANES, tq), F32),
                        pltpu.VMEM((n_hc * V7X_SUBLANES, tq), F32),
                        pltpu.VMEM((n_hc, tk, tq), BF16),
                        pltpu.VMEM((n_hc, tq), F32),
                        pltpu.VMEM((n_hc, 2 * dk + ONES_ROWS, tq), F32)],
        compiler_params=pltpu.CompilerParams(
            dimension_semantics=("arbitrary", "arbitrary"), vmem_limit_bytes=_vmem_limit(est)),
        name="diff_attn",
    )(qt, k, vt, yabd, x, wo, lamv, subg, post_g)


def _mem_proj_kernel(mem_ref, g_ref, wk_ref, wv_ref, kt_ref, v_ref):
    mn = _rms(mem_ref[...], g_ref[...])
    kt_ref[...] = _dot(mn, wk_ref[...]).T.astype(BF16)
    v_ref[...] = _dot(mn, wv_ref[...]).astype(BF16)


def _mem_proj(mem, layer, g, wk, wv):
    b, m, d = mem.shape
    const = lambda bi: (layer, 0, 0)
    est = 2 * 2 * d * d * 4 + 8 * m * d * 4
    return pl.pallas_call(
        _mem_proj_kernel,
        grid=(b,),
        in_specs=[
            pl.BlockSpec((None, m, d), lambda bi: (bi, 0, 0)),
            pl.BlockSpec((None, 1, d), const),
            pl.BlockSpec((None, d, d), const),
            pl.BlockSpec((None, d, d), const),
        ],
        out_specs=[
            pl.BlockSpec((None, d, m), lambda bi: (bi, 0, 0)),
            pl.BlockSpec((None, m, d), lambda bi: (bi, 0, 0)),
        ],
        out_shape=[jax.ShapeDtypeStruct((b, d, m), BF16), jax.ShapeDtypeStruct((b, m, d), BF16)],
        compiler_params=pltpu.CompilerParams(
            dimension_semantics=("arbitrary",), vmem_limit_bytes=_vmem_limit(est)),
        name="mem_proj",
    )(mem, g, wk, wv)


def _xattn_kernel(x_ref, pre_g_ref, wq_ref, kt_ref, v_ref, wo_ref, post_g_ref, o_ref,
                  q_ref, ox_ref):
    x = x_ref[...]
    tm, d = x.shape
    hd = d // X_HEADS
    h = _rms(x, pre_g_ref[...])
    for head in range(X_HEADS):
        cols = slice(head * hd, (head + 1) * hd)
        q_ref[:, cols] = _dot(h, wq_ref[:, cols]).astype(BF16)
    for head in range(X_HEADS):
        cols = slice(head * hd, (head + 1) * hd)
        for r0 in range(0, tm, XATTN_ROWS):
            rows = slice(r0, r0 + XATTN_ROWS)
            sc = _dot(q_ref[rows, cols], kt_ref[cols, :]) * (hd ** -0.5)
            p = jnp.exp(sc - jnp.max(sc, axis=-1, keepdims=True))
            l = jnp.sum(p, axis=-1, keepdims=True)
            ox_ref[rows, cols] = (_dot(p.astype(BF16), v_ref[:, cols]) / l).astype(BF16)
    o_ref[...] = x + _rms(_dot(ox_ref[...], wo_ref[...]), post_g_ref[...])


def _xattn(x, layer, s, tm, pre_g, wq, kt, v, wo, post_g):
    n, d = x.shape
    m = v.shape[1]
    nt = s // tm
    const = lambda i: (layer, 0, 0)
    est = 2 * d * d * (4 + 2) + 4 * d * m * 2 + 8 * tm * d * 4
    return pl.pallas_call(
        _xattn_kernel,
        grid=(n // tm,),
        in_specs=[
            pl.BlockSpec((tm, d), lambda i: (i, 0)),
            pl.BlockSpec((None, 1, d), const),
            pl.BlockSpec((None, d, d), const),
            pl.BlockSpec((None, d, m), lambda i: (i // nt, 0, 0)),
            pl.BlockSpec((None, m, d), lambda i: (i // nt, 0, 0)),
            pl.BlockSpec((None, d, d), const),
            pl.BlockSpec((None, 1, d), const),
        ],
        out_specs=pl.BlockSpec((tm, d), lambda i: (i, 0)),
        out_shape=jax.ShapeDtypeStruct((n, d), F32),
        scratch_shapes=[pltpu.VMEM((tm, d), BF16), pltpu.VMEM((tm, d), BF16)],
        compiler_params=pltpu.CompilerParams(
            dimension_semantics=("arbitrary",), vmem_limit_bytes=_vmem_limit(est)),
        name="xattn",
    )(x, pre_g, wq, kt, v, wo, post_g)


def _rope_tables(positions, dk, rot):
    half = rot // 2
    inv_freq = ROPE_THETA ** (-jnp.arange(0, rot, 2, dtype=F32) / rot)
    ang = positions.astype(F32).reshape(-1, 1) * inv_freq
    d = jnp.arange(V7X_LANES) % dk
    pick = ((d[None, :] % half == jnp.arange(half)[:, None]) & (d[None, :] < rot)).astype(F32)
    sign = jnp.where(d < half, -1.0, 1.0).astype(F32)
    expand = functools.partial(jnp.dot, precision=lax.Precision.HIGHEST)
    cos_t = expand(jnp.cos(ang), pick) + (d >= rot).astype(F32)
    sin_t = expand(jnp.sin(ang), pick * sign)
    return cos_t, sin_t


def kernel(x, mem, positions, f1_pre_g, f1_w_gate, f1_w_up, f1_w_down, f1_post_g, mix_pre_g, w_in, pool_w, pool_scale, sconv_w, lam_q1, lam_k1, lam_q2, lam_k2, diff_subln_g, conf_dw_w, conf_dw_b, conf_ln_g, conf_ln_b, w_o, mix_post_g, x_pre_g, x_mem_g, x_wq, x_wk, x_wv, x_wo, x_post_g, f2_pre_g, f2_w_gate, f2_w_up, f2_w_down, f2_post_g):
    b, s, d = x.shape
    depth = w_in.shape[0]
    gw = d // N_MIXERS
    dk = gw // DIFF_HEADS // 2
    rot = dk // 4
    tm = _row_tile(s)
    assert s % (2 * V7X_MXU_DIM) == 0 and V7X_MXU_DIM % CHUNK == 0 and tm >= HALO
    assert HALO >= conf_dw_w.shape[1] - 1 and HALO >= max(POOL_WINDOWS) - 1
    assert tm % CONV_ROWS == 0 and tm % XATTN_ROWS == 0

    bf = lambda w: w.astype(BF16)
    row = lambda g: g[:, None, :]
    ct, st = _rope_tables(positions, dk, rot)
    n_pg = pool_w.shape[1]
    poolw = bf(jnp.einsum("lgcd,gh->lgchd", pool_w, jnp.eye(n_pg, dtype=F32)).reshape(depth, gw, gw))
    lam_init = jnp.asarray([0.8 - 0.6 * math.exp(-0.3 * l) for l in range(depth)], F32)
    lamv = jnp.concatenate(
        [jnp.stack([lam_q1, lam_k1, lam_q2, lam_k2], axis=1).astype(F32),
         jnp.broadcast_to(lam_init[:, None, None], (depth, 1, dk)),
         jnp.zeros((depth, 3, dk), F32)], axis=1)
    subg = jnp.tile(diff_subln_g, (1, DIFF_HEADS))[:, :, None]
    f1 = (row(f1_pre_g), f1_w_gate, f1_w_up, f1_w_down, row(f1_post_g))
    f2 = (row(f2_pre_g), f2_w_gate, f2_w_up, f2_w_down, row(f2_post_g))
    w_o_b, wo_b = bf(w_o), bf(x_wo)

    xf = x.reshape(b * s, d)
    for l in range(depth):
        xf = _ffn(xf, l, *f1, tm)
        yabd, qt, k, vt = _mix_in(xf, l, b, s, tm, row(mix_pre_g), w_in, ct, st, poolw,
                                  row(pool_scale), sconv_w, conf_dw_w, row(conf_dw_b),
                                  row(conf_ln_g), row(conf_ln_b))
        xf = _diff_attn(xf, l, b, s, qt, k, vt, yabd, w_o_b, lamv, subg, row(mix_post_g))
        kt, v = _mem_proj(mem, l, row(x_mem_g), x_wk, x_wv)
        xf = _xattn(xf, l, s, tm, row(x_pre_g), x_wq, kt, v, wo_b, row(x_post_g))
        xf = _ffn(xf, l, *f2, tm)
    return xf.reshape(b, s, d)
```

```python
import functools
import math

import jax
import jax.numpy as jnp
from jax import lax
from jax.experimental import pallas as pl
from jax.experimental.pallas import tpu as pltpu

F32 = jnp.float32
BF16 = jnp.bfloat16

EPS = 1e-6
CHUNK = 64
N_MIXERS = 4
POOL_WINDOWS = (2, 4, 8, 16)
DIFF_HEADS = 4
ROPE_THETA = 500000.0
X_HEADS = 4

V7X_LANES = 128
V7X_SUBLANES = 8
V7X_MXU_DIM = 256
V7X_VMEM_BYTES = 64 * 1024 * 1024

V7X_BF16_SUBLANES = 16

HALO = 32
ONES_ROWS = V7X_BF16_SUBLANES
CONV_ROWS = 64
XATTN_ROWS = 128
XATTN_SUBTILE = 512
NEG_INF = float("-inf")
LOG2E = math.log2(math.e)


VMEM_SPILL_ALLOWANCE = 12 << 20
VMEM_RESERVED = 6 << 20


def _vmem_limit(estimate_bytes):
    return int(min(estimate_bytes + VMEM_SPILL_ALLOWANCE, V7X_VMEM_BYTES - VMEM_RESERVED))


def _rms(x, g):
    return x * lax.rsqrt(jnp.mean(x * x, axis=-1, keepdims=True) + EPS) * g


def _dot(a, b):
    return jnp.dot(a, b, preferred_element_type=F32)


def _row_tile(rows_per_seq):
    for cand in (512, 256, 128):
        if rows_per_seq % cand == 0:
            return cand
    raise ValueError(f"sequence length {rows_per_seq} must be a multiple of 128")


def _ffn_kernel(x_ref, pre_g_ref, wg_ref, wu_ref, wd_ref, post_g_ref, o_ref):
    x = x_ref[...]
    h = _rms(x, pre_g_ref[...])
    g = _dot(h, wg_ref[...])
    u = _dot(h, wu_ref[...])
    y = _dot(g * jax.nn.sigmoid(g) * u, wd_ref[...])
    o_ref[...] = x + 0.5 * _rms(y, post_g_ref[...])


def _ffn(x, layer, pre_g, wg, wu, wd, post_g, tm):
    n, d = x.shape
    f = wg.shape[-1]
    const = lambda i: (layer, 0, 0)
    est = 3 * d * f * 4 + 4 * tm * d * 4 + 2 * tm * f * 4
    return pl.pallas_call(
        _ffn_kernel,
        grid=(n // tm,),
        in_specs=[
            pl.BlockSpec((tm, d), lambda i: (i, 0)),
            pl.BlockSpec((None, 1, d), const),
            pl.BlockSpec((None, d, f), const),
            pl.BlockSpec((None, d, f), const),
            pl.BlockSpec((None, f, d), const),
            pl.BlockSpec((None, 1, d), const),
        ],
        out_specs=pl.BlockSpec((tm, d), lambda i: (i, 0)),
        out_shape=jax.ShapeDtypeStruct((n, d), F32),
        compiler_params=pltpu.CompilerParams(
            dimension_semantics=("arbitrary",), vmem_limit_bytes=_vmem_limit(est)),
        name="ffn",
    )(x, pre_g, wg, wu, wd, post_g)


def _mix_in_kernel(x_ref, g_ref, w_in_ref, ct_ref, st_ref, poolw_ref, pscale_ref,
                   sconv_ref, confw_ref, confb_ref, lng_ref, lnb_ref,
                   yabd_ref, qt_ref, k_ref, vt_ref, ext_ref, dly_ref, *, gw, dk, rot):
    tm = x_ref.shape[0]
    t_idx = pl.program_id(1)
    sub = V7X_SUBLANES
    dv = 2 * dk
    pool_ch = gw // len(POOL_WINDOWS)
    n_s = sconv_ref.shape[0]
    n_c = confw_ref.shape[0]
    reps = gw // V7X_LANES
    erows = slice(HALO, HALO + tm)

    @pl.when(t_idx == 0)
    def _():
        ext_ref[0:HALO, :] = jnp.zeros((HALO, ext_ref.shape[1]), F32)

    h = _rms(x_ref[...], g_ref[...])

    def proj(c):
        return _dot(h, w_in_ref[:, c * gw:(c + 1) * gw])

    def hist(col, shift):
        return ext_ref[HALO - shift:HALO - shift + tm, col * gw:(col + 1) * gw]

    def pool_mixer():
        ext_ref[erows, 0:gw] = proj(0)
        frame = lax.broadcasted_iota(jnp.int32, (tm, V7X_LANES), 0) + t_idx * tm
        lane = lax.broadcasted_iota(jnp.int32, (tm, V7X_LANES), 1)
        halves = []
        for hv in range(reps):
            groups = [g for g in range(len(POOL_WINDOWS)) if (g * pool_ch) // V7X_LANES == hv]
            wins = [POOL_WINDOWS[g] for g in groups]
            lo = hv * V7X_LANES
            cur = ext_ref[erows, lo:lo + V7X_LANES]
            run = cur
            prefix = {1: run}
            for j in range(1, max(wins)):
                run = run + ext_ref[HALO - j:HALO - j + tm, lo:lo + V7X_LANES]
                prefix[j + 1] = run
            csum = prefix[wins[-1]]
            win = jnp.full((tm, V7X_LANES), wins[-1], jnp.int32)
            for g, w in zip(groups[:-1][::-1], wins[:-1][::-1]):
                sel = lane < (g + 1) * pool_ch - lo
                csum = jnp.where(sel, prefix[w], csum)
                win = jnp.where(sel, w, win)
            cnt = jnp.minimum(frame + 1, win).astype(F32)
            halves.append(csum / cnt - cur)
        mixed = jnp.concatenate(halves, axis=1).astype(BF16)
        yabd_ref[:, 0:gw] = (_dot(mixed, poolw_ref[...]) * pscale_ref[...]).astype(BF16)

    def sconv_mixer():
        xs = proj(1)
        bg = proj(2)
        cg = proj(3)
        ext_ref[erows, gw:2 * gw] = cg * xs
        conv = sconv_ref[n_s - 1:n_s, :] * hist(1, 0)
        for j in range(n_s - 1):
            conv = conv + sconv_ref[j:j + 1, :] * hist(1, n_s - 1 - j)
        yabd_ref[:, gw:2 * gw] = (bg * conv).astype(BF16)

    def rope(v):
        lane_q = lax.broadcasted_iota(jnp.int32, (tm, gw), 1)
        first_half = (lane_q % dk) < (rot // 2)
        cos_t = jnp.concatenate([ct_ref[...]] * reps, axis=1)
        sin_t = jnp.concatenate([st_ref[...]] * reps, axis=1)
        partner = jnp.where(first_half,
                            pltpu.roll(v, gw - rot // 2, axis=1),
                            pltpu.roll(v, rot // 2, axis=1))
        return v * cos_t + partner * sin_t

    def q_out():
        qt_ref[...] = (rope(proj(4)) * (dk ** -0.5 * LOG2E)).T.astype(BF16)

    def k_out():
        k_ref[...] = rope(proj(5)).astype(BF16)

    def v_out():
        vt = proj(6).T.astype(BF16)
        for head in range(gw // dv):
            base = head * (dv + ONES_ROWS)
            vt_ref[base:base + dv, :] = vt[head * dv:(head + 1) * dv, :]
            vt_ref[base + dv:base + dv + ONES_ROWS, :] = jnp.ones((ONES_ROWS, tm), BF16)

    ext_ref[erows, 2 * gw:3 * gw] = proj(7) * jax.nn.sigmoid(proj(8))
    for r in range(1, sub):
        dly_ref[r - 1] = ext_ref[sub - r:sub - r + dly_ref.shape[1], 2 * gw:3 * gw]
    others = [q_out, k_out, v_out, sconv_mixer, pool_mixer]
    for c0 in range(0, tm, CONV_ROWS):
        u = confb_ref[...]
        for j in range(n_c):
            a, r = divmod(n_c - 1 - j, sub)
            if r == 0:
                lo = HALO - sub * a + c0
                src = ext_ref[lo:lo + CONV_ROWS, 2 * gw:3 * gw]
            else:
                lo = HALO - sub - sub * a + c0
                src = dly_ref[r - 1, lo:lo + CONV_ROWS, :]
            u = u + confw_ref[j:j + 1, :] * src
        mu = jnp.mean(u, axis=-1, keepdims=True)
        uc = u - mu
        ln = (uc * lax.rsqrt(jnp.mean(uc * uc, axis=-1, keepdims=True) + EPS) * lng_ref[...]
              + lnb_ref[...])
        yabd_ref[c0:c0 + CONV_ROWS, 2 * gw:3 * gw] = (ln * jax.nn.sigmoid(ln)).astype(BF16)
        if others:
            others.pop(0)()
    for mixer in others:
        mixer()

    ext_ref[0:HALO, :] = ext_ref[tm:tm + HALO, :]


def _mix_in(x, layer, b, s, tm, pre_g, w_in, ct, st, poolw, pscale, sconv_w, conf_w,
            conf_b, ln_g, ln_b):
    n, d = x.shape
    gw = d // N_MIXERS
    in_cols = w_in.shape[-1]
    dk = gw // DIFF_HEADS // 2
    rot = dk // 4
    nt = s // tm
    vt_rows = DIFF_HEADS * (2 * dk + ONES_ROWS)
    row = lambda bi, ti: (bi * nt + ti, 0)
    const = lambda bi, ti: (layer, 0, 0)
    est = (2 * d * in_cols * 4 + 4 * tm * d * 4 + (HALO + tm) * (3 + V7X_SUBLANES - 1) * gw * 4
           + 12 * tm * gw * 4)
    kern = functools.partial(_mix_in_kernel, gw=gw, dk=dk, rot=rot)
    return pl.pallas_call(
        kern,
        grid=(b, nt),
        in_specs=[
            pl.BlockSpec((tm, d), row),
            pl.BlockSpec((None, 1, d), const),
            pl.BlockSpec((None, d, in_cols), const),
            pl.BlockSpec((tm, V7X_LANES), row),
            pl.BlockSpec((tm, V7X_LANES), row),
            pl.BlockSpec((None, gw, gw), const),
            pl.BlockSpec((None, 1, gw), const),
            pl.BlockSpec((None,) + sconv_w.shape[1:], const),
            pl.BlockSpec((None,) + conf_w.shape[1:], const),
            pl.BlockSpec((None, 1, gw), const),
            pl.BlockSpec((None, 1, gw), const),
            pl.BlockSpec((None, 1, gw), const),
        ],
        out_specs=[
            pl.BlockSpec((tm, 3 * gw), row),
            pl.BlockSpec((None, gw, tm), lambda bi, ti: (bi, 0, ti)),
            pl.BlockSpec((None, tm, gw), lambda bi, ti: (bi, ti, 0)),
            pl.BlockSpec((None, vt_rows, tm), lambda bi, ti: (bi, 0, ti)),
        ],
        out_shape=[
            jax.ShapeDtypeStruct((n, 3 * gw), BF16),
            jax.ShapeDtypeStruct((b, gw, s), BF16),
            jax.ShapeDtypeStruct((b, s, gw), BF16),
            jax.ShapeDtypeStruct((b, vt_rows, s), BF16),
        ],
        scratch_shapes=[pltpu.VMEM((HALO + tm, 3 * gw), F32),
                        pltpu.VMEM((V7X_SUBLANES - 1, HALO - V7X_SUBLANES + tm, gw), F32)],
        compiler_params=pltpu.CompilerParams(
            dimension_semantics=("arbitrary", "arbitrary"), vmem_limit_bytes=_vmem_limit(est)),
        name="mix_in",
    )(x, pre_g, w_in, ct, st, poolw, pscale, sconv_w, conf_w, conf_b, ln_g, ln_b)


def _diff_attn_kernel(qt_ref, k_ref, vt_ref, yabd_ref, x_ref, wo_ref, lam_ref, subg_ref,
                      post_g_ref, o_ref, qs_ref, sa_ref, sb_ref, cma_ref, cmb_ref, p_ref, m_ref,
                      acc_ref, *, dk, tk, rc):
    gw, tq = qt_ref.shape
    dv = 2 * dk
    dva = dv + ONES_ROWS
    n_hc = gw // dk
    n_rc = tk // rc
    kpq = tq // tk
    qi = pl.program_id(1)

    for hc in range(n_hc):
        qs_ref[hc] = jnp.zeros((gw, tq), BF16)
        qs_ref[hc, hc * dk:(hc + 1) * dk, :] = qt_ref[hc * dk:(hc + 1) * dk, :]
    acc_ref[...] = jnp.zeros(acc_ref.shape, F32)
    m_ref[...] = jnp.full(m_ref.shape, NEG_INF, F32)

    def fold8(v, op):
        return op(v.reshape(v.shape[0] // V7X_SUBLANES, V7X_SUBLANES, v.shape[1]), axis=0)

    def rows8(hc):
        return slice(hc * V7X_SUBLANES, (hc + 1) * V7X_SUBLANES)

    def scores(blk, s_ref, cm_ref, q0=0):
        kb = k_ref[pl.ds(pl.multiple_of(blk * tk, tk), tk), :]
        for hc in range(n_hc):
            sc = _dot(kb, qs_ref[hc, :, q0:])
            s_ref[hc, :, q0:] = sc
            if cm_ref is not None:
                cm_ref[rows8(hc), q0:] = fold8(sc, jnp.max)

    def softmax_pv(blk, s_ref, cm_ref, diag=None, q0=0):
        off = pl.multiple_of(blk * tk, tk)
        masked = diag is not None
        qchunk = (lax.broadcasted_iota(jnp.int32, (1, tq - q0), 1) + q0) // CHUNK

        def chunk(hc, r):
            sv = s_ref[hc, r * rc:(r + 1) * rc, q0:]
            if masked:
                kc = (diag * tk + r * rc) // CHUNK
                if kc > q0 // CHUNK:
                    sv = jnp.where(qchunk >= kc, sv, NEG_INF)
            return sv

        for hc in range(n_hc):
            head = hc // 2
            if masked:
                cm = fold8(chunk(hc, 0), jnp.max)
                for r in range(1, n_rc):
                    cm = jnp.maximum(cm, fold8(chunk(hc, r), jnp.max))
            else:
                cm = cm_ref[rows8(hc), q0:]
            m_old = m_ref[hc:hc + 1, q0:]
            m_new = jnp.maximum(m_old, jnp.max(cm, axis=0, keepdims=True))
            alpha = jnp.exp2(m_old - m_new)
            m_ref[hc:hc + 1, q0:] = m_new
            for r in range(n_rc):
                p_ref[hc, r * rc:(r + 1) * rc, q0:] = jnp.exp2(chunk(hc, r) - m_new).astype(BF16)
            vt = vt_ref[head * dva:(head + 1) * dva, pl.ds(off, tk)]
            acc_ref[hc, :, q0:] = alpha * acc_ref[hc, :, q0:] + _dot(vt, p_ref[hc, :, q0:])

    assert kpq == 2
    scores(2 * qi, sa_ref, None)
    scores(2 * qi + 1, sb_ref, None, q0=tk)
    softmax_pv(2 * qi, sa_ref, None, diag=0)
    softmax_pv(2 * qi + 1, sb_ref, None, diag=1, q0=tk)

    def pair(pi, carry):
        scores(2 * pi, sa_ref, cma_ref)
        scores(2 * pi + 1, sb_ref, cmb_ref)
        softmax_pv(2 * pi, sa_ref, cma_ref)
        softmax_pv(2 * pi + 1, sb_ref, cmb_ref)
        return carry

    lax.fori_loop(0, qi, pair, 0)

    lam_init = lam_ref[4:5, 0:1]
    lam = (jnp.exp(jnp.sum(lam_ref[0:1, :] * lam_ref[1:2, :], axis=1, keepdims=True))
           - jnp.exp(jnp.sum(lam_ref[2:3, :] * lam_ref[3:4, :], axis=1, keepdims=True))
           + lam_init)
    heads = []
    for head in range(n_hc // 2):
        o = (acc_ref[2 * head, 0:dv, :] / acc_ref[2 * head, dv:dv + 1, :]
             - lam * (acc_ref[2 * head + 1, 0:dv, :] / acc_ref[2 * head + 1, dv:dv + 1, :]))
        ms = jnp.mean(o * o, axis=0, keepdims=True)
        heads.append(o * lax.rsqrt(ms + EPS) * subg_ref[head * dv:(head + 1) * dv, :] * (1.0 - lam_init))
    yc = jnp.concatenate(heads, axis=0).T.astype(BF16)

    y = (_dot(yabd_ref[:, 0:2 * gw], wo_ref[0:2 * gw, :])
         + _dot(yc, wo_ref[2 * gw:3 * gw, :])
         + _dot(yabd_ref[:, 2 * gw:3 * gw], wo_ref[3 * gw:4 * gw, :]))
    o_ref[...] = x_ref[...] + _rms(y, post_g_ref[...])


def _diff_attn(x, layer, b, s, qt, k, vt, yabd, wo, lamv, subg, post_g):
    n, d = x.shape
    gw = d // N_MIXERS
    dk = gw // DIFF_HEADS // 2
    tk = V7X_MXU_DIM
    tq = 2 * tk
    rc = 32
    nq = s // tq
    row = lambda bi, qi: (bi * nq + qi, 0)
    const = lambda bi, qi: (layer, 0, 0)
    n_hc = gw // dk
    est = (2 * s * (gw + vt.shape[1]) * 2 + 2 * d * d * 2 + 4 * tq * d * 4 + n_hc * gw * tq * 2
           + n_hc * tk * tq * (4 + 4 + 2) + 4 * tq * d * 4)
    kern = functools.partial(_diff_attn_kernel, dk=dk, tk=tk, rc=rc)
    return pl.pallas_call(
        kern,
        grid=(b, nq),
        in_specs=[
            pl.BlockSpec((None, gw, tq), lambda bi, qi: (bi, 0, qi)),
            pl.BlockSpec((None, s, gw), lambda bi, qi: (bi, 0, 0)),
            pl.BlockSpec((None, vt.shape[1], s), lambda bi, qi: (bi, 0, 0)),
            pl.BlockSpec((tq, 3 * gw), row),
            pl.BlockSpec((tq, d), row),
            pl.BlockSpec((None, d, d), const),
            pl.BlockSpec((None,) + lamv.shape[1:], const),
            pl.BlockSpec((None, gw, 1), const),
            pl.BlockSpec((None, 1, d), const),
        ],
        out_specs=pl.BlockSpec((tq, d), row),
        out_shape=jax.ShapeDtypeStruct((n, d), F32),
        scratch_shapes=[pltpu.VMEM((n_hc, gw, tq), BF16),
                        pltpu.VMEM((n_hc, tk, tq), F32),
                        pltpu.VMEM((n_hc, tk, tq), F32),
                        pltpu.VMEM((n_hc * V7X_SUBLANES, tq), F32),
                        pltpu.VMEM((n_hc * V7X_SUBLANES, tq), F32),
                        pltpu.VMEM((n_hc, tk, tq), BF16),
                        pltpu.VMEM((n_hc, tq), F32),
                        pltpu.VMEM((n_hc, 2 * dk + ONES_ROWS, tq), F32)],
        compiler_params=pltpu.CompilerParams(
            dimension_semantics=("arbitrary", "arbitrary"), vmem_limit_bytes=_vmem_limit(est)),
        name="diff_attn",
    )(qt, k, vt, yabd, x, wo, lamv, subg, post_g)


def _mem_proj_kernel(mem_ref, g_ref, wk_ref, wv_ref, kt_ref, v_ref):
    mn = _rms(mem_ref[...], g_ref[...])
    kt_ref[...] = _dot(mn, wk_ref[...]).T.astype(BF16)
    v_ref[...] = _dot(mn, wv_ref[...]).astype(BF16)


def _mem_proj(mem, layer, g, wk, wv):
    b, m, d = mem.shape
    const = lambda bi: (layer, 0, 0)
    est = 2 * 2 * d * d * 4 + 8 * m * d * 4
    return pl.pallas_call(
        _mem_proj_kernel,
        grid=(b,),
        in_specs=[
            pl.BlockSpec((None, m, d), lambda bi: (bi, 0, 0)),
            pl.BlockSpec((None, 1, d), const),
            pl.BlockSpec((None, d, d), const),
            pl.BlockSpec((None, d, d), const),
        ],
        out_specs=[
            pl.BlockSpec((None, d, m), lambda bi: (bi, 0, 0)),
            pl.BlockSpec((None, m, d), lambda bi: (bi, 0, 0)),
        ],
        out_shape=[jax.ShapeDtypeStruct((b, d, m), BF16), jax.ShapeDtypeStruct((b, m, d), BF16)],
        compiler_params=pltpu.CompilerParams(
            dimension_semantics=("arbitrary",), vmem_limit_bytes=_vmem_limit(est)),
        name="mem_proj",
    )(mem, g, wk, wv)


def _xattn_kernel(x_ref, pre_g_ref, wq_ref, kt_ref, v_ref, wo_ref, post_g_ref, o_ref,
                  q_ref, ox_ref):
    tm, d = x_ref.shape
    hd = d // X_HEADS
    for t0 in range(0, tm, XATTN_SUBTILE):
        trows = slice(t0, t0 + XATTN_SUBTILE)
        x = x_ref[trows, :]
        h = _rms(x, pre_g_ref[...])
        for head in range(X_HEADS):
            cols = slice(head * hd, (head + 1) * hd)
            q_ref[trows, cols] = _dot(h, wq_ref[:, cols]).astype(BF16)
        for head in range(X_HEADS):
            cols = slice(head * hd, (head + 1) * hd)
            for r0 in range(t0, t0 + XATTN_SUBTILE, XATTN_ROWS):
                rows = slice(r0, r0 + XATTN_ROWS)
                sc = _dot(q_ref[rows, cols], kt_ref[cols, :]) * (hd ** -0.5)
                p = jnp.exp(sc - jnp.max(sc, axis=-1, keepdims=True))
                l = jnp.sum(p, axis=-1, keepdims=True)
                ox_ref[rows, cols] = (_dot(p.astype(BF16), v_ref[:, cols]) / l).astype(BF16)
        o_ref[trows, :] = x + _rms(_dot(ox_ref[trows, :], wo_ref[...]), post_g_ref[...])


def _xattn(x, layer, s, tm, pre_g, wq, kt, v, wo, post_g):
    n, d = x.shape
    m = v.shape[1]
    nt = s // tm
    const = lambda i: (layer, 0, 0)
    est = 2 * d * d * (4 + 2) + 4 * d * m * 2 + 8 * tm * d * 4
    return pl.pallas_call(
        _xattn_kernel,
        grid=(n // tm,),
        in_specs=[
            pl.BlockSpec((tm, d), lambda i: (i, 0)),
            pl.BlockSpec((None, 1, d), const),
            pl.BlockSpec((None, d, d), const),
            pl.BlockSpec((None, d, m), lambda i: (i // nt, 0, 0)),
            pl.BlockSpec((None, m, d), lambda i: (i // nt, 0, 0)),
            pl.BlockSpec((None, d, d), const),
            pl.BlockSpec((None, 1, d), const),
        ],
        out_specs=pl.BlockSpec((tm, d), lambda i: (i, 0)),
        out_shape=jax.ShapeDtypeStruct((n, d), F32),
        scratch_shapes=[pltpu.VMEM((tm, d), BF16), pltpu.VMEM((tm, d), BF16)],
        compiler_params=pltpu.CompilerParams(
            dimension_semantics=("arbitrary",), vmem_limit_bytes=_vmem_limit(est)),
        name="xattn",
    )(x, pre_g, wq, kt, v, wo, post_g)


def _rope_tables(positions, dk, rot):
    half = rot // 2
    inv_freq = ROPE_THETA ** (-jnp.arange(0, rot, 2, dtype=F32) / rot)
    ang = positions.astype(F32).reshape(-1, 1) * inv_freq
    d = jnp.arange(V7X_LANES) % dk
    pick = ((d[None, :] % half == jnp.arange(half)[:, None]) & (d[None, :] < rot)).astype(F32)
    sign = jnp.where(d < half, -1.0, 1.0).astype(F32)
    expand = functools.partial(jnp.dot, precision=lax.Precision.HIGHEST)
    cos_t = expand(jnp.cos(ang), pick) + (d >= rot).astype(F32)
    sin_t = expand(jnp.sin(ang), pick * sign)
    return cos_t, sin_t


def kernel(x, mem, positions, f1_pre_g, f1_w_gate, f1_w_up, f1_w_down, f1_post_g, mix_pre_g, w_in, pool_w, pool_scale, sconv_w, lam_q1, lam_k1, lam_q2, lam_k2, diff_subln_g, conf_dw_w, conf_dw_b, conf_ln_g, conf_ln_b, w_o, mix_post_g, x_pre_g, x_mem_g, x_wq, x_wk, x_wv, x_wo, x_post_g, f2_pre_g, f2_w_gate, f2_w_up, f2_w_down, f2_post_g):
    b, s, d = x.shape
    depth = w_in.shape[0]
    gw = d // N_MIXERS
    dk = gw // DIFF_HEADS // 2
    rot = dk // 4
    tm = _row_tile(s)
    assert s % (2 * V7X_MXU_DIM) == 0 and V7X_MXU_DIM % CHUNK == 0 and tm >= HALO
    assert HALO >= conf_dw_w.shape[1] - 1 and HALO >= max(POOL_WINDOWS) - 1
    assert tm % CONV_ROWS == 0 and s % (2 * XATTN_SUBTILE) == 0 and XATTN_SUBTILE % XATTN_ROWS == 0

    bf = lambda w: w.astype(BF16)
    row = lambda g: g[:, None, :]
    ct, st = _rope_tables(positions, dk, rot)
    n_pg = pool_w.shape[1]
    poolw = bf(jnp.einsum("lgcd,gh->lgchd", pool_w, jnp.eye(n_pg, dtype=F32)).reshape(depth, gw, gw))
    lam_init = jnp.asarray([0.8 - 0.6 * math.exp(-0.3 * l) for l in range(depth)], F32)
    lamv = jnp.concatenate(
        [jnp.stack([lam_q1, lam_k1, lam_q2, lam_k2], axis=1).astype(F32),
         jnp.broadcast_to(lam_init[:, None, None], (depth, 1, dk)),
         jnp.zeros((depth, 3, dk), F32)], axis=1)
    subg = jnp.tile(diff_subln_g, (1, DIFF_HEADS))[:, :, None]
    f1 = (row(f1_pre_g), f1_w_gate, f1_w_up, f1_w_down, row(f1_post_g))
    f2 = (row(f2_pre_g), f2_w_gate, f2_w_up, f2_w_down, row(f2_post_g))
    w_o_b, wo_b = bf(w_o), bf(x_wo)

    xf = x.reshape(b * s, d)
    for l in range(depth):
        xf = _ffn(xf, l, *f1, tm)
        yabd, qt, k, vt = _mix_in(xf, l, b, s, tm, row(mix_pre_g), w_in, ct, st, poolw,
                                  row(pool_scale), sconv_w, conf_dw_w, row(conf_dw_b),
                                  row(conf_ln_g), row(conf_ln_b))
        xf = _diff_attn(xf, l, b, s, qt, k, vt, yabd, w_o_b, lamv, subg, row(mix_post_g))
        kt, v = _mem_proj(mem, l, row(x_mem_g), x_wk, x_wv)
        xf = _xattn(xf, l, s, 2 * XATTN_SUBTILE, row(x_pre_g), x_wq, kt, v, wo_b, row(x_post_g))
        xf = _ffn(xf, l, *f2, tm)
    return xf.reshape(b, s, d)
```

```python
import functools
import math

import jax
import jax.numpy as jnp
from jax import lax
from jax.experimental import pallas as pl
from jax.experimental.pallas import tpu as pltpu

F32 = jnp.float32
BF16 = jnp.bfloat16

EPS = 1e-6
CHUNK = 64
N_MIXERS = 4
POOL_WINDOWS = (2, 4, 8, 16)
DIFF_HEADS = 4
ROPE_THETA = 500000.0
X_HEADS = 4

V7X_LANES = 128
V7X_SUBLANES = 8
V7X_MXU_DIM = 256
V7X_VMEM_BYTES = 64 * 1024 * 1024

V7X_BF16_SUBLANES = 16

HALO = 32
ONES_ROWS = V7X_BF16_SUBLANES
CONV_ROWS = 64
XATTN_ROWS = 128
XATTN_SUBTILE = 512
NEG_INF = float("-inf")
LOG2E = math.log2(math.e)


VMEM_SPILL_ALLOWANCE = 12 << 20
VMEM_RESERVED = 6 << 20


def _vmem_limit(estimate_bytes):
    return int(min(estimate_bytes + VMEM_SPILL_ALLOWANCE, V7X_VMEM_BYTES - VMEM_RESERVED))


def _rms(x, g):
    return x * lax.rsqrt(jnp.mean(x * x, axis=-1, keepdims=True) + EPS) * g


def _dot(a, b):
    return jnp.dot(a, b, preferred_element_type=F32)


def _row_tile(rows_per_seq):
    for cand in (512, 256, 128):
        if rows_per_seq % cand == 0:
            return cand
    raise ValueError(f"sequence length {rows_per_seq} must be a multiple of 128")


def _ffn_kernel(x_ref, pre_g_ref, wg_ref, wu_ref, wd_ref, post_g_ref, o_ref):
    x = x_ref[...]
    h = _rms(x, pre_g_ref[...])
    g = _dot(h, wg_ref[...])
    u = _dot(h, wu_ref[...])
    y = _dot(g * jax.nn.sigmoid(g) * u, wd_ref[...])
    o_ref[...] = x + 0.5 * _rms(y, post_g_ref[...])


def _ffn(x, layer, pre_g, wg, wu, wd, post_g, tm):
    n, d = x.shape
    f = wg.shape[-1]
    const = lambda i: (layer, 0, 0)
    est = 3 * d * f * 4 + 4 * tm * d * 4 + 2 * tm * f * 4
    return pl.pallas_call(
        _ffn_kernel,
        grid=(n // tm,),
        in_specs=[
            pl.BlockSpec((tm, d), lambda i: (i, 0)),
            pl.BlockSpec((None, 1, d), const),
            pl.BlockSpec((None, d, f), const),
            pl.BlockSpec((None, d, f), const),
            pl.BlockSpec((None, f, d), const),
            pl.BlockSpec((None, 1, d), const),
        ],
        out_specs=pl.BlockSpec((tm, d), lambda i: (i, 0)),
        out_shape=jax.ShapeDtypeStruct((n, d), F32),
        compiler_params=pltpu.CompilerParams(
            dimension_semantics=("arbitrary",), vmem_limit_bytes=_vmem_limit(est)),
        name="ffn",
    )(x, pre_g, wg, wu, wd, post_g)


def _mix_in_kernel(x_ref, g_ref, w_in_ref, ct_ref, st_ref, poolw_ref, pscale_ref,
                   sconv_ref, confw_ref, confb_ref, lng_ref, lnb_ref,
                   yabd_ref, qt_ref, k_ref, vt_ref, ext_ref, dly_ref, *, gw, dk, rot):
    tm = x_ref.shape[0]
    t_idx = pl.program_id(1)
    sub = V7X_SUBLANES
    dv = 2 * dk
    pool_ch = gw // len(POOL_WINDOWS)
    n_s = sconv_ref.shape[0]
    n_c = confw_ref.shape[0]
    reps = gw // V7X_LANES
    erows = slice(HALO, HALO + tm)

    @pl.when(t_idx == 0)
    def _():
        ext_ref[0:HALO, :] = jnp.zeros((HALO, ext_ref.shape[1]), F32)

    h = _rms(x_ref[...], g_ref[...])

    def proj(c):
        return _dot(h, w_in_ref[:, c * gw:(c + 1) * gw])

    def hist(col, shift):
        return ext_ref[HALO - shift:HALO - shift + tm, col * gw:(col + 1) * gw]

    def pool_mixer():
        ext_ref[erows, 0:gw] = proj(0)
        frame = lax.broadcasted_iota(jnp.int32, (tm, V7X_LANES), 0) + t_idx * tm
        lane = lax.broadcasted_iota(jnp.int32, (tm, V7X_LANES), 1)
        halves = []
        for hv in range(reps):
            groups = [g for g in range(len(POOL_WINDOWS)) if (g * pool_ch) // V7X_LANES == hv]
            wins = [POOL_WINDOWS[g] for g in groups]
            lo = hv * V7X_LANES
            cur = ext_ref[erows, lo:lo + V7X_LANES]
            run = cur
            prefix = {1: run}
            for j in range(1, max(wins)):
                run = run + ext_ref[HALO - j:HALO - j + tm, lo:lo + V7X_LANES]
                prefix[j + 1] = run
            csum = prefix[wins[-1]]
            win = jnp.full((tm, V7X_LANES), wins[-1], jnp.int32)
            for g, w in zip(groups[:-1][::-1], wins[:-1][::-1]):
                sel = lane < (g + 1) * pool_ch - lo
                csum = jnp.where(sel, prefix[w], csum)
                win = jnp.where(sel, w, win)
            cnt = jnp.minimum(frame + 1, win).astype(F32)
            halves.append(csum / cnt - cur)
        mixed = jnp.concatenate(halves, axis=1).astype(BF16)
        yabd_ref[:, 0:gw] = (_dot(mixed, poolw_ref[...]) * pscale_ref[...]).astype(BF16)

    def sconv_mixer():
        xs = proj(1)
        bg = proj(2)
        cg = proj(3)
        ext_ref[erows, gw:2 * gw] = cg * xs
        conv = sconv_ref[n_s - 1:n_s, :] * hist(1, 0)
        for j in range(n_s - 1):
            conv = conv + sconv_ref[j:j + 1, :] * hist(1, n_s - 1 - j)
        yabd_ref[:, gw:2 * gw] = (bg * conv).astype(BF16)

    def rope(v):
        lane_q = lax.broadcasted_iota(jnp.int32, (tm, gw), 1)
        first_half = (lane_q % dk) < (rot // 2)
        cos_t = jnp.concatenate([ct_ref[...]] * reps, axis=1)
        sin_t = jnp.concatenate([st_ref[...]] * reps, axis=1)
        partner = jnp.where(first_half,
                            pltpu.roll(v, gw - rot // 2, axis=1),
                            pltpu.roll(v, rot // 2, axis=1))
        return v * cos_t + partner * sin_t

    def q_out():
        qt_ref[...] = (rope(proj(4)) * (dk ** -0.5 * LOG2E)).T.astype(BF16)

    def k_out():
        k_ref[...] = rope(proj(5)).astype(BF16)

    def v_out():
        vt = proj(6).T.astype(BF16)
        for head in range(gw // dv):
            base = head * (dv + ONES_ROWS)
            vt_ref[base:base + dv, :] = vt[head * dv:(head + 1) * dv, :]
            vt_ref[base + dv:base + dv + ONES_ROWS, :] = jnp.ones((ONES_ROWS, tm), BF16)

    ext_ref[erows, 2 * gw:3 * gw] = proj(7) * jax.nn.sigmoid(proj(8))
    for r in range(1, sub):
        dly_ref[r - 1] = ext_ref[sub - r:sub - r + dly_ref.shape[1], 2 * gw:3 * gw]
    others = [q_out, k_out, v_out, sconv_mixer, pool_mixer]
    for c0 in range(0, tm, CONV_ROWS):
        u = confb_ref[...]
        for j in range(n_c):
            a, r = divmod(n_c - 1 - j, sub)
            if r == 0:
                lo = HALO - sub * a + c0
                src = ext_ref[lo:lo + CONV_ROWS, 2 * gw:3 * gw]
            else:
                lo = HALO - sub - sub * a + c0
                src = dly_ref[r - 1, lo:lo + CONV_ROWS, :]
            u = u + confw_ref[j:j + 1, :] * src
        mu = jnp.mean(u, axis=-1, keepdims=True)
        uc = u - mu
        ln = (uc * lax.rsqrt(jnp.mean(uc * uc, axis=-1, keepdims=True) + EPS) * lng_ref[...]
              + lnb_ref[...])
        yabd_ref[c0:c0 + CONV_ROWS, 2 * gw:3 * gw] = (ln * jax.nn.sigmoid(ln)).astype(BF16)
        if others:
            others.pop(0)()
    for mixer in others:
        mixer()

    ext_ref[0:HALO, :] = ext_ref[tm:tm + HALO, :]


def _mix_in(x, layer, b, s, tm, pre_g, w_in, ct, st, poolw, pscale, sconv_w, conf_w,
            conf_b, ln_g, ln_b):
    n, d = x.shape
    gw = d // N_MIXERS
    in_cols = w_in.shape[-1]
    dk = gw // DIFF_HEADS // 2
    rot = dk // 4
    nt = s // tm
    vt_rows = DIFF_HEADS * (2 * dk + ONES_ROWS)
    row = lambda bi, ti: (bi * nt + ti, 0)
    const = lambda bi, ti: (layer, 0, 0)
    est = (2 * d * in_cols * 4 + 4 * tm * d * 4 + (HALO + tm) * (3 + V7X_SUBLANES - 1) * gw * 4
           + 12 * tm * gw * 4)
    kern = functools.partial(_mix_in_kernel, gw=gw, dk=dk, rot=rot)
    return pl.pallas_call(
        kern,
        grid=(b, nt),
        in_specs=[
            pl.BlockSpec((tm, d), row),
            pl.BlockSpec((None, 1, d), const),
            pl.BlockSpec((None, d, in_cols), const),
            pl.BlockSpec((tm, V7X_LANES), row),
            pl.BlockSpec((tm, V7X_LANES), row),
            pl.BlockSpec((None, gw, gw), const),
            pl.BlockSpec((None, 1, gw), const),
            pl.BlockSpec((None,) + sconv_w.shape[1:], const),
            pl.BlockSpec((None,) + conf_w.shape[1:], const),
            pl.BlockSpec((None, 1, gw), const),
            pl.BlockSpec((None, 1, gw), const),
            pl.BlockSpec((None, 1, gw), const),
        ],
        out_specs=[
            pl.BlockSpec((tm, 3 * gw), row),
            pl.BlockSpec((None, gw, tm), lambda bi, ti: (bi, 0, ti)),
            pl.BlockSpec((None, tm, gw), lambda bi, ti: (bi, ti, 0)),
            pl.BlockSpec((None, vt_rows, tm), lambda bi, ti: (bi, 0, ti)),
        ],
        out_shape=[
            jax.ShapeDtypeStruct((n, 3 * gw), BF16),
            jax.ShapeDtypeStruct((b, gw, s), BF16),
            jax.ShapeDtypeStruct((b, s, gw), BF16),
            jax.ShapeDtypeStruct((b, vt_rows, s), BF16),
        ],
        scratch_shapes=[pltpu.VMEM((HALO + tm, 3 * gw), F32),
                        pltpu.VMEM((V7X_SUBLANES - 1, HALO - V7X_SUBLANES + tm, gw), F32)],
        compiler_params=pltpu.CompilerParams(
            dimension_semantics=("arbitrary", "arbitrary"), vmem_limit_bytes=_vmem_limit(est)),
        name="mix_in",
    )(x, pre_g, w_in, ct, st, poolw, pscale, sconv_w, conf_w, conf_b, ln_g, ln_b)


def _diff_attn_kernel(qt_ref, k_ref, vt_ref, yabd_ref, x_ref, wo_ref, lam_ref, subg_ref,
                      post_g_ref, o_ref, qs_ref, sa_ref, sb_ref, cma_ref, cmb_ref, p_ref, m_ref,
                      acc_ref, *, dk, tk, rc):
    gw, tq = qt_ref.shape
    dv = 2 * dk
    dva = dv + ONES_ROWS
    n_hc = gw // dk
    n_rc = tk // rc
    kpq = tq // tk
    qi = pl.program_id(1)

    for hc in range(n_hc):
        qs_ref[hc] = jnp.zeros((gw, tq), BF16)
        qs_ref[hc, hc * dk:(hc + 1) * dk, :] = qt_ref[hc * dk:(hc + 1) * dk, :]
    acc_ref[...] = jnp.zeros(acc_ref.shape, F32)
    m_ref[...] = jnp.full(m_ref.shape, NEG_INF, F32)

    def fold8(v, op):
        return op(v.reshape(v.shape[0] // V7X_SUBLANES, V7X_SUBLANES, v.shape[1]), axis=0)

    def rows8(hc):
        return slice(hc * V7X_SUBLANES, (hc + 1) * V7X_SUBLANES)

    def scores(blk, s_ref, cm_ref, q0=0):
        kb = k_ref[pl.ds(pl.multiple_of(blk * tk, tk), tk), :]
        for hc in range(n_hc):
            sc = _dot(kb, qs_ref[hc, :, q0:])
            s_ref[hc, :, q0:] = sc
            if cm_ref is not None:
                cm_ref[rows8(hc), q0:] = fold8(sc, jnp.max)

    def softmax_pv(blk, s_ref, cm_ref, diag=None, q0=0):
        off = pl.multiple_of(blk * tk, tk)
        masked = diag is not None
        qchunk = (lax.broadcasted_iota(jnp.int32, (1, tq - q0), 1) + q0) // CHUNK

        def chunk(hc, r):
            sv = s_ref[hc, r * rc:(r + 1) * rc, q0:]
            if masked:
                kc = (diag * tk + r * rc) // CHUNK
                if kc > q0 // CHUNK:
                    sv = jnp.where(qchunk >= kc, sv, NEG_INF)
            return sv

        alphas = []
        for hc in range(n_hc):
            head, comp = divmod(hc, 2)
            if masked:
                cm = fold8(chunk(hc, 0), jnp.max)
                for r in range(1, n_rc):
                    cm = jnp.maximum(cm, fold8(chunk(hc, r), jnp.max))
            else:
                cm = cm_ref[rows8(hc), q0:]
            m_old = m_ref[hc:hc + 1, q0:]
            m_new = jnp.maximum(m_old, jnp.max(cm, axis=0, keepdims=True))
            alphas.append(jnp.exp2(m_old - m_new))
            m_ref[hc:hc + 1, q0:] = m_new
            for r in range(n_rc):
                p_ref[head, r * rc:(r + 1) * rc, comp * tq + q0:(comp + 1) * tq] = jnp.exp2(
                    chunk(hc, r) - m_new).astype(BF16)
            if comp == 1:
                vt = vt_ref[head * dva:(head + 1) * dva, pl.ds(off, tk)]
                if q0 == 0:
                    pv = _dot(vt, p_ref[head])
                    for c in range(2):
                        acc_ref[2 * head + c] = (alphas[2 * head + c] * acc_ref[2 * head + c]
                                                 + pv[:, c * tq:(c + 1) * tq])
                else:
                    for c in range(2):
                        pv = _dot(vt, p_ref[head, :, c * tq + q0:(c + 1) * tq])
                        acc_ref[2 * head + c, :, q0:] = (
                            alphas[2 * head + c] * acc_ref[2 * head + c, :, q0:] + pv)

    assert kpq == 2
    scores(2 * qi, sa_ref, None)
    scores(2 * qi + 1, sb_ref, None, q0=tk)
    softmax_pv(2 * qi, sa_ref, None, diag=0)
    softmax_pv(2 * qi + 1, sb_ref, None, diag=1, q0=tk)

    def pair(pi, carry):
        scores(2 * pi, sa_ref, cma_ref)
        scores(2 * pi + 1, sb_ref, cmb_ref)
        softmax_pv(2 * pi, sa_ref, cma_ref)
        softmax_pv(2 * pi + 1, sb_ref, cmb_ref)
        return carry

    lax.fori_loop(0, qi, pair, 0)

    lam_init = lam_ref[4:5, 0:1]
    lam = (jnp.exp(jnp.sum(lam_ref[0:1, :] * lam_ref[1:2, :], axis=1, keepdims=True))
           - jnp.exp(jnp.sum(lam_ref[2:3, :] * lam_ref[3:4, :], axis=1, keepdims=True))
           + lam_init)
    heads = []
    for head in range(n_hc // 2):
        o = (acc_ref[2 * head, 0:dv, :] / acc_ref[2 * head, dv:dv + 1, :]
             - lam * (acc_ref[2 * head + 1, 0:dv, :] / acc_ref[2 * head + 1, dv:dv + 1, :]))
        ms = jnp.mean(o * o, axis=0, keepdims=True)
        heads.append(o * lax.rsqrt(ms + EPS) * subg_ref[head * dv:(head + 1) * dv, :] * (1.0 - lam_init))
    yc = jnp.concatenate(heads, axis=0).T.astype(BF16)

    y = (_dot(yabd_ref[:, 0:2 * gw], wo_ref[0:2 * gw, :])
         + _dot(yc, wo_ref[2 * gw:3 * gw, :])
         + _dot(yabd_ref[:, 2 * gw:3 * gw], wo_ref[3 * gw:4 * gw, :]))
    o_ref[...] = x_ref[...] + _rms(y, post_g_ref[...])


def _diff_attn(x, layer, b, s, qt, k, vt, yabd, wo, lamv, subg, post_g):
    n, d = x.shape
    gw = d // N_MIXERS
    dk = gw // DIFF_HEADS // 2
    tk = V7X_MXU_DIM
    tq = 2 * tk
    rc = 32
    nq = s // tq
    row = lambda bi, qi: (bi * nq + qi, 0)
    const = lambda bi, qi: (layer, 0, 0)
    n_hc = gw // dk
    est = (2 * s * (gw + vt.shape[1]) * 2 + 2 * d * d * 2 + 4 * tq * d * 4 + n_hc * gw * tq * 2
           + n_hc * tk * tq * (4 + 4 + 2) + 4 * tq * d * 4)
    kern = functools.partial(_diff_attn_kernel, dk=dk, tk=tk, rc=rc)
    return pl.pallas_call(
        kern,
        grid=(b, nq),
        in_specs=[
            pl.BlockSpec((None, gw, tq), lambda bi, qi: (bi, 0, qi)),
            pl.BlockSpec((None, s, gw), lambda bi, qi: (bi, 0, 0)),
            pl.BlockSpec((None, vt.shape[1], s), lambda bi, qi: (bi, 0, 0)),
            pl.BlockSpec((tq, 3 * gw), row),
            pl.BlockSpec((tq, d), row),
            pl.BlockSpec((None, d, d), const),
            pl.BlockSpec((None,) + lamv.shape[1:], const),
            pl.BlockSpec((None, gw, 1), const),
            pl.BlockSpec((None, 1, d), const),
        ],
        out_specs=pl.BlockSpec((tq, d), row),
        out_shape=jax.ShapeDtypeStruct((n, d), F32),
        scratch_shapes=[pltpu.VMEM((n_hc, gw, tq), BF16),
                        pltpu.VMEM((n_hc, tk, tq), F32),
                        pltpu.VMEM((n_hc, tk, tq), F32),
                        pltpu.VMEM((n_hc * V7X_SUBLANES, tq), F32),
                        pltpu.VMEM((n_hc * V7X_SUBLANES, tq), F32),
                        pltpu.VMEM((n_hc // 2, tk, 2 * tq), BF16),
                        pltpu.VMEM((n_hc, tq), F32),
                        pltpu.VMEM((n_hc, 2 * dk + ONES_ROWS, tq), F32)],
        compiler_params=pltpu.CompilerParams(
            dimension_semantics=("arbitrary", "arbitrary"), vmem_limit_bytes=_vmem_limit(est)),
        name="diff_attn",
    )(qt, k, vt, yabd, x, wo, lamv, subg, post_g)


def _mem_proj_kernel(mem_ref, g_ref, wk_ref, wv_ref, kt_ref, v_ref):
    mn = _rms(mem_ref[...], g_ref[...])
    kt_ref[...] = _dot(mn, wk_ref[...]).T.astype(BF16)
    v_ref[...] = _dot(mn, wv_ref[...]).astype(BF16)


def _mem_proj(mem, layer, g, wk, wv):
    b, m, d = mem.shape
    const = lambda bi: (layer, 0, 0)
    est = 2 * 2 * d * d * 4 + 8 * m * d * 4
    return pl.pallas_call(
        _mem_proj_kernel,
        grid=(b,),
        in_specs=[
            pl.BlockSpec((None, m, d), lambda bi: (bi, 0, 0)),
            pl.BlockSpec((None, 1, d), const),
            pl.BlockSpec((None, d, d), const),
            pl.BlockSpec((None, d, d), const),
        ],
        out_specs=[
            pl.BlockSpec((None, d, m), lambda bi: (bi, 0, 0)),
            pl.BlockSpec((None, m, d), lambda bi: (bi, 0, 0)),
        ],
        out_shape=[jax.ShapeDtypeStruct((b, d, m), BF16), jax.ShapeDtypeStruct((b, m, d), BF16)],
        compiler_params=pltpu.CompilerParams(
            dimension_semantics=("arbitrary",), vmem_limit_bytes=_vmem_limit(est)),
        name="mem_proj",
    )(mem, g, wk, wv)


def _xattn_kernel(x_ref, pre_g_ref, wq_ref, kt_ref, v_ref, wo_ref, post_g_ref, o_ref,
                  q_ref, ox_ref):
    tm, d = x_ref.shape
    hd = d // X_HEADS
    for t0 in range(0, tm, XATTN_SUBTILE):
        trows = slice(t0, t0 + XATTN_SUBTILE)
        x = x_ref[trows, :]
        h = _rms(x, pre_g_ref[...])
        for head in range(X_HEADS):
            cols = slice(head * hd, (head + 1) * hd)
            q_ref[trows, cols] = _dot(h, wq_ref[:, cols]).astype(BF16)
        for head in range(X_HEADS):
            cols = slice(head * hd, (head + 1) * hd)
            for r0 in range(t0, t0 + XATTN_SUBTILE, XATTN_ROWS):
                rows = slice(r0, r0 + XATTN_ROWS)
                sc = _dot(q_ref[rows, cols], kt_ref[cols, :]) * (hd ** -0.5)
                p = jnp.exp(sc - jnp.max(sc, axis=-1, keepdims=True))
                l = jnp.sum(p, axis=-1, keepdims=True)
                ox_ref[rows, cols] = (_dot(p.astype(BF16), v_ref[:, cols]) / l).astype(BF16)
        o_ref[trows, :] = x + _rms(_dot(ox_ref[trows, :], wo_ref[...]), post_g_ref[...])


def _xattn(x, layer, s, tm, pre_g, wq, kt, v, wo, post_g):
    n, d = x.shape
    m = v.shape[1]
    nt = s // tm
    const = lambda i: (layer, 0, 0)
    est = 2 * d * d * (4 + 2) + 4 * d * m * 2 + 8 * tm * d * 4
    return pl.pallas_call(
        _xattn_kernel,
        grid=(n // tm,),
        in_specs=[
            pl.BlockSpec((tm, d), lambda i: (i, 0)),
            pl.BlockSpec((None, 1, d), const),
            pl.BlockSpec((None, d, d), const),
            pl.BlockSpec((None, d, m), lambda i: (i // nt, 0, 0)),
            pl.BlockSpec((None, m, d), lambda i: (i // nt, 0, 0)),
            pl.BlockSpec((None, d, d), const),
            pl.BlockSpec((None, 1, d), const),
        ],
        out_specs=pl.BlockSpec((tm, d), lambda i: (i, 0)),
        out_shape=jax.ShapeDtypeStruct((n, d), F32),
        scratch_shapes=[pltpu.VMEM((tm, d), BF16), pltpu.VMEM((tm, d), BF16)],
        compiler_params=pltpu.CompilerParams(
            dimension_semantics=("arbitrary",), vmem_limit_bytes=_vmem_limit(est)),
        name="xattn",
    )(x, pre_g, wq, kt, v, wo, post_g)


def _rope_tables(positions, dk, rot):
    half = rot // 2
    inv_freq = ROPE_THETA ** (-jnp.arange(0, rot, 2, dtype=F32) / rot)
    ang = positions.astype(F32).reshape(-1, 1) * inv_freq
    d = jnp.arange(V7X_LANES) % dk
    pick = ((d[None, :] % half == jnp.arange(half)[:, None]) & (d[None, :] < rot)).astype(F32)
    sign = jnp.where(d < half, -1.0, 1.0).astype(F32)
    expand = functools.partial(jnp.dot, precision=lax.Precision.HIGHEST)
    cos_t = expand(jnp.cos(ang), pick) + (d >= rot).astype(F32)
    sin_t = expand(jnp.sin(ang), pick * sign)
    return cos_t, sin_t


def kernel(x, mem, positions, f1_pre_g, f1_w_gate, f1_w_up, f1_w_down, f1_post_g, mix_pre_g, w_in, pool_w, pool_scale, sconv_w, lam_q1, lam_k1, lam_q2, lam_k2, diff_subln_g, conf_dw_w, conf_dw_b, conf_ln_g, conf_ln_b, w_o, mix_post_g, x_pre_g, x_mem_g, x_wq, x_wk, x_wv, x_wo, x_post_g, f2_pre_g, f2_w_gate, f2_w_up, f2_w_down, f2_post_g):
    b, s, d = x.shape
    depth = w_in.shape[0]
    gw = d // N_MIXERS
    dk = gw // DIFF_HEADS // 2
    rot = dk // 4
    tm = _row_tile(s)
    assert s % (2 * V7X_MXU_DIM) == 0 and V7X_MXU_DIM % CHUNK == 0 and tm >= HALO
    assert HALO >= conf_dw_w.shape[1] - 1 and HALO >= max(POOL_WINDOWS) - 1
    assert tm % CONV_ROWS == 0 and s % (2 * XATTN_SUBTILE) == 0 and XATTN_SUBTILE % XATTN_ROWS == 0

    bf = lambda w: w.astype(BF16)
    row = lambda g: g[:, None, :]
    ct, st = _rope_tables(positions, dk, rot)
    n_pg = pool_w.shape[1]
    poolw = bf(jnp.einsum("lgcd,gh->lgchd", pool_w, jnp.eye(n_pg, dtype=F32)).reshape(depth, gw, gw))
    lam_init = jnp.asarray([0.8 - 0.6 * math.exp(-0.3 * l) for l in range(depth)], F32)
    lamv = jnp.concatenate(
        [jnp.stack([lam_q1, lam_k1, lam_q2, lam_k2], axis=1).astype(F32),
         jnp.broadcast_to(lam_init[:, None, None], (depth, 1, dk)),
         jnp.zeros((depth, 3, dk), F32)], axis=1)
    subg = jnp.tile(diff_subln_g, (1, DIFF_HEADS))[:, :, None]
    f1 = (row(f1_pre_g), f1_w_gate, f1_w_up, f1_w_down, row(f1_post_g))
    f2 = (row(f2_pre_g), f2_w_gate, f2_w_up, f2_w_down, row(f2_post_g))
    w_o_b, wo_b = bf(w_o), bf(x_wo)

    xf = x.reshape(b * s, d)
    for l in range(depth):
        xf = _ffn(xf, l, *f1, tm)
        yabd, qt, k, vt = _mix_in(xf, l, b, s, tm, row(mix_pre_g), w_in, ct, st, poolw,
                                  row(pool_scale), sconv_w, conf_dw_w, row(conf_dw_b),
                                  row(conf_ln_g), row(conf_ln_b))
        xf = _diff_attn(xf, l, b, s, qt, k, vt, yabd, w_o_b, lamv, subg, row(mix_post_g))
        kt, v = _mem_proj(mem, l, row(x_mem_g), x_wk, x_wv)
        xf = _xattn(xf, l, s, 2 * XATTN_SUBTILE, row(x_pre_g), x_wq, kt, v, wo_b, row(x_post_g))
        xf = _ffn(xf, l, *f2, tm)
    return xf.reshape(b, s, d)
```

```python
import functools
import math

import jax
import jax.numpy as jnp
from jax import lax
from jax.experimental import pallas as pl
from jax.experimental.pallas import tpu as pltpu

F32 = jnp.float32
BF16 = jnp.bfloat16

EPS = 1e-6
CHUNK = 64
N_MIXERS = 4
POOL_WINDOWS = (2, 4, 8, 16)
DIFF_HEADS = 4
ROPE_THETA = 500000.0
X_HEADS = 4

V7X_LANES = 128
V7X_SUBLANES = 8
V7X_MXU_DIM = 256
V7X_VMEM_BYTES = 64 * 1024 * 1024

V7X_BF16_SUBLANES = 16

HALO = 32
ONES_ROWS = V7X_BF16_SUBLANES
CONV_ROWS = 64
XATTN_ROWS = 128
XATTN_SUBTILE = 512
NEG_INF = float("-inf")
LOG2E = math.log2(math.e)


VMEM_SPILL_ALLOWANCE = 12 << 20
VMEM_RESERVED = 6 << 20


def _vmem_limit(estimate_bytes):
    return int(min(estimate_bytes + VMEM_SPILL_ALLOWANCE, V7X_VMEM_BYTES - VMEM_RESERVED))


def _rms(x, g):
    return x * lax.rsqrt(jnp.mean(x * x, axis=-1, keepdims=True) + EPS) * g


def _dot(a, b):
    return jnp.dot(a, b, preferred_element_type=F32)


def _row_tile(rows_per_seq):
    for cand in (512, 256, 128):
        if rows_per_seq % cand == 0:
            return cand
    raise ValueError(f"sequence length {rows_per_seq} must be a multiple of 128")


def _ffn_kernel(x_ref, pre_g_ref, wg_ref, wu_ref, wd_ref, post_g_ref, o_ref):
    x = x_ref[...]
    h = _rms(x, pre_g_ref[...])
    g = _dot(h, wg_ref[...])
    u = _dot(h, wu_ref[...])
    y = _dot(g * jax.nn.sigmoid(g) * u, wd_ref[...])
    o_ref[...] = x + 0.5 * _rms(y, post_g_ref[...])


def _ffn(x, layer, pre_g, wg, wu, wd, post_g, tm):
    n, d = x.shape
    f = wg.shape[-1]
    const = lambda i: (layer, 0, 0)
    est = 3 * d * f * 4 + 4 * tm * d * 4 + 2 * tm * f * 4
    return pl.pallas_call(
        _ffn_kernel,
        grid=(n // tm,),
        in_specs=[
            pl.BlockSpec((tm, d), lambda i: (i, 0)),
            pl.BlockSpec((None, 1, d), const),
            pl.BlockSpec((None, d, f), const),
            pl.BlockSpec((None, d, f), const),
            pl.BlockSpec((None, f, d), const),
            pl.BlockSpec((None, 1, d), const),
        ],
        out_specs=pl.BlockSpec((tm, d), lambda i: (i, 0)),
        out_shape=jax.ShapeDtypeStruct((n, d), F32),
        compiler_params=pltpu.CompilerParams(
            dimension_semantics=("arbitrary",), vmem_limit_bytes=_vmem_limit(est)),
        name="ffn",
    )(x, pre_g, wg, wu, wd, post_g)


def _mix_in_kernel(x_ref, g_ref, w_in_ref, ct_ref, st_ref, poolw_ref, pscale_ref,
                   sconv_ref, confw_ref, confb_ref, lng_ref, lnb_ref,
                   yabd_ref, qt_ref, k_ref, vt_ref, ext_ref, dly_ref, *, gw, dk, rot):
    tm = x_ref.shape[0]
    t_idx = pl.program_id(1)
    sub = V7X_SUBLANES
    dv = 2 * dk
    pool_ch = gw // len(POOL_WINDOWS)
    n_s = sconv_ref.shape[0]
    n_c = confw_ref.shape[0]
    reps = gw // V7X_LANES
    erows = slice(HALO, HALO + tm)

    @pl.when(t_idx == 0)
    def _():
        ext_ref[0:HALO, :] = jnp.zeros((HALO, ext_ref.shape[1]), F32)

    h = _rms(x_ref[...], g_ref[...])

    def proj(c):
        return _dot(h, w_in_ref[:, c * gw:(c + 1) * gw])

    def hist(col, shift):
        return ext_ref[HALO - shift:HALO - shift + tm, col * gw:(col + 1) * gw]

    def pool_mixer():
        ext_ref[erows, 0:gw] = proj(0)
        frame = lax.broadcasted_iota(jnp.int32, (tm, V7X_LANES), 0) + t_idx * tm
        lane = lax.broadcasted_iota(jnp.int32, (tm, V7X_LANES), 1)
        halves = []
        for hv in range(reps):
            groups = [g for g in range(len(POOL_WINDOWS)) if (g * pool_ch) // V7X_LANES == hv]
            wins = [POOL_WINDOWS[g] for g in groups]
            lo = hv * V7X_LANES
            cur = ext_ref[erows, lo:lo + V7X_LANES]
            run = cur
            prefix = {1: run}
            for j in range(1, max(wins)):
                run = run + ext_ref[HALO - j:HALO - j + tm, lo:lo + V7X_LANES]
                prefix[j + 1] = run
            csum = prefix[wins[-1]]
            win = jnp.full((tm, V7X_LANES), wins[-1], jnp.int32)
            for g, w in zip(groups[:-1][::-1], wins[:-1][::-1]):
                sel = lane < (g + 1) * pool_ch - lo
                csum = jnp.where(sel, prefix[w], csum)
                win = jnp.where(sel, w, win)
            cnt = jnp.minimum(frame + 1, win).astype(F32)
            halves.append(csum / cnt - cur)
        mixed = jnp.concatenate(halves, axis=1).astype(BF16)
        yabd_ref[:, 0:gw] = (_dot(mixed, poolw_ref[...]) * pscale_ref[...]).astype(BF16)

    def sconv_mixer():
        xs = proj(1)
        bg = proj(2)
        cg = proj(3)
        ext_ref[erows, gw:2 * gw] = cg * xs
        conv = sconv_ref[n_s - 1:n_s, :] * hist(1, 0)
        for j in range(n_s - 1):
            conv = conv + sconv_ref[j:j + 1, :] * hist(1, n_s - 1 - j)
        yabd_ref[:, gw:2 * gw] = (bg * conv).astype(BF16)

    def rope(v):
        lane_q = lax.broadcasted_iota(jnp.int32, (tm, gw), 1)
        first_half = (lane_q % dk) < (rot // 2)
        cos_t = jnp.concatenate([ct_ref[...]] * reps, axis=1)
        sin_t = jnp.concatenate([st_ref[...]] * reps, axis=1)
        partner = jnp.where(first_half,
                            pltpu.roll(v, gw - rot // 2, axis=1),
                            pltpu.roll(v, rot // 2, axis=1))
        return v * cos_t + partner * sin_t

    def q_out():
        qt_ref[...] = (rope(proj(4)) * (dk ** -0.5 * LOG2E)).T.astype(BF16)

    def k_out():
        k_ref[...] = rope(proj(5)).astype(BF16)

    def v_out():
        vt = proj(6).T.astype(BF16)
        for head in range(gw // dv):
            base = head * (dv + ONES_ROWS)
            vt_ref[base:base + dv, :] = vt[head * dv:(head + 1) * dv, :]
            vt_ref[base + dv:base + dv + ONES_ROWS, :] = jnp.ones((ONES_ROWS, tm), BF16)

    ext_ref[erows, 2 * gw:3 * gw] = proj(7) * jax.nn.sigmoid(proj(8))
    for r in range(1, sub):
        dly_ref[r - 1] = ext_ref[sub - r:sub - r + dly_ref.shape[1], 2 * gw:3 * gw]
    others = [q_out, k_out, v_out, sconv_mixer, pool_mixer]
    for c0 in range(0, tm, CONV_ROWS):
        u = confb_ref[...]
        for j in range(n_c):
            a, r = divmod(n_c - 1 - j, sub)
            if r == 0:
                lo = HALO - sub * a + c0
                src = ext_ref[lo:lo + CONV_ROWS, 2 * gw:3 * gw]
            else:
                lo = HALO - sub - sub * a + c0
                src = dly_ref[r - 1, lo:lo + CONV_ROWS, :]
            u = u + confw_ref[j:j + 1, :] * src
        mu = jnp.mean(u, axis=-1, keepdims=True)
        uc = u - mu
        ln = (uc * lax.rsqrt(jnp.mean(uc * uc, axis=-1, keepdims=True) + EPS) * lng_ref[...]
              + lnb_ref[...])
        yabd_ref[c0:c0 + CONV_ROWS, 2 * gw:3 * gw] = (ln * jax.nn.sigmoid(ln)).astype(BF16)
        if others:
            others.pop(0)()
    for mixer in others:
        mixer()

    ext_ref[0:HALO, :] = ext_ref[tm:tm + HALO, :]


def _mix_in(x, layer, b, s, tm, pre_g, w_in, ct, st, poolw, pscale, sconv_w, conf_w,
            conf_b, ln_g, ln_b):
    n, d = x.shape
    gw = d // N_MIXERS
    in_cols = w_in.shape[-1]
    dk = gw // DIFF_HEADS // 2
    rot = dk // 4
    nt = s // tm
    vt_rows = DIFF_HEADS * (2 * dk + ONES_ROWS)
    row = lambda bi, ti: (bi * nt + ti, 0)
    const = lambda bi, ti: (layer, 0, 0)
    est = (2 * d * in_cols * 4 + 4 * tm * d * 4 + (HALO + tm) * (3 + V7X_SUBLANES - 1) * gw * 4
           + 12 * tm * gw * 4)
    kern = functools.partial(_mix_in_kernel, gw=gw, dk=dk, rot=rot)
    return pl.pallas_call(
        kern,
        grid=(b, nt),
        in_specs=[
            pl.BlockSpec((tm, d), row),
            pl.BlockSpec((None, 1, d), const),
            pl.BlockSpec((None, d, in_cols), const),
            pl.BlockSpec((tm, V7X_LANES), row),
            pl.BlockSpec((tm, V7X_LANES), row),
            pl.BlockSpec((None, gw, gw), const),
            pl.BlockSpec((None, 1, gw), const),
            pl.BlockSpec((None,) + sconv_w.shape[1:], const),
            pl.BlockSpec((None,) + conf_w.shape[1:], const),
            pl.BlockSpec((None, 1, gw), const),
            pl.BlockSpec((None, 1, gw), const),
            pl.BlockSpec((None, 1, gw), const),
        ],
        out_specs=[
            pl.BlockSpec((tm, 3 * gw), row),
            pl.BlockSpec((None, gw, tm), lambda bi, ti: (bi, 0, ti)),
            pl.BlockSpec((None, tm, gw), lambda bi, ti: (bi, ti, 0)),
            pl.BlockSpec((None, vt_rows, tm), lambda bi, ti: (bi, 0, ti)),
        ],
        out_shape=[
            jax.ShapeDtypeStruct((n, 3 * gw), BF16),
            jax.ShapeDtypeStruct((b, gw, s), BF16),
            jax.ShapeDtypeStruct((b, s, gw), BF16),
            jax.ShapeDtypeStruct((b, vt_rows, s), BF16),
        ],
        scratch_shapes=[pltpu.VMEM((HALO + tm, 3 * gw), F32),
                        pltpu.VMEM((V7X_SUBLANES - 1, HALO - V7X_SUBLANES + tm, gw), F32)],
        compiler_params=pltpu.CompilerParams(
            dimension_semantics=("arbitrary", "arbitrary"), vmem_limit_bytes=_vmem_limit(est)),
        name="mix_in",
    )(x, pre_g, w_in, ct, st, poolw, pscale, sconv_w, conf_w, conf_b, ln_g, ln_b)


def _diff_attn_kernel(qt_ref, k_ref, vt_ref, yabd_ref, x_ref, wo_ref, lam_ref, subg_ref,
                      post_g_ref, o_ref, qs_ref, sa_ref, sb_ref, sc_ref, sd_ref, cma_ref, cmb_ref,
                      cmc_ref, cmd_ref, p_ref, m_ref, acc_ref, *, dk, tk, rc):
    gw, tq = qt_ref.shape
    dv = 2 * dk
    dva = dv + ONES_ROWS
    n_hc = gw // dk
    n_rc = tk // rc
    kpq = tq // tk
    qi = pl.program_id(1)

    for hc in range(n_hc):
        qs_ref[hc] = jnp.zeros((gw, tq), BF16)
        qs_ref[hc, hc * dk:(hc + 1) * dk, :] = qt_ref[hc * dk:(hc + 1) * dk, :]
    acc_ref[...] = jnp.zeros(acc_ref.shape, F32)
    m_ref[...] = jnp.full(m_ref.shape, NEG_INF, F32)

    def fold8(v, op):
        return op(v.reshape(v.shape[0] // V7X_SUBLANES, V7X_SUBLANES, v.shape[1]), axis=0)

    def rows8(hc):
        return slice(hc * V7X_SUBLANES, (hc + 1) * V7X_SUBLANES)

    def strips(only):
        return range(n_hc) if only is None else (only,)

    def scores(blk, s_ref, cm_ref, q0=0, only=None):
        kb = k_ref[pl.ds(pl.multiple_of(blk * tk, tk), tk), :]
        for hc in strips(only):
            sc = _dot(kb, qs_ref[hc, :, q0:])
            s_ref[hc, :, q0:] = sc
            if cm_ref is not None:
                cm_ref[rows8(hc), q0:] = fold8(sc, jnp.max)

    def softmax_pv(blk, s_ref, cm_ref, diag=None, q0=0, only=None, pbuf=0):
        off = pl.multiple_of(blk * tk, tk)
        masked = diag is not None
        qchunk = (lax.broadcasted_iota(jnp.int32, (1, tq - q0), 1) + q0) // CHUNK

        def chunk(hc, r):
            sv = s_ref[hc, r * rc:(r + 1) * rc, q0:]
            if masked:
                kc = (diag * tk + r * rc) // CHUNK
                if kc > q0 // CHUNK:
                    sv = jnp.where(qchunk >= kc, sv, NEG_INF)
            return sv

        for hc in strips(only):
            head = hc // 2
            if masked:
                cm = fold8(chunk(hc, 0), jnp.max)
                for r in range(1, n_rc):
                    cm = jnp.maximum(cm, fold8(chunk(hc, r), jnp.max))
            else:
                cm = cm_ref[rows8(hc), q0:]
            m_old = m_ref[hc:hc + 1, q0:]
            m_new = jnp.maximum(m_old, jnp.max(cm, axis=0, keepdims=True))
            alpha = jnp.exp2(m_old - m_new)
            m_ref[hc:hc + 1, q0:] = m_new
            for r in range(n_rc):
                p_ref[pbuf, hc, r * rc:(r + 1) * rc, q0:] = jnp.exp2(chunk(hc, r) - m_new).astype(BF16)
            vt = vt_ref[head * dva:(head + 1) * dva, pl.ds(off, tk)]
            acc_ref[hc, :, q0:] = alpha * acc_ref[hc, :, q0:] + _dot(vt, p_ref[pbuf, hc, :, q0:])

    assert kpq == 2
    scores(2 * qi, sa_ref, None)
    scores(2 * qi + 1, sb_ref, None, q0=tk)
    softmax_pv(2 * qi, sa_ref, None, diag=0)
    softmax_pv(2 * qi + 1, sb_ref, None, diag=1, q0=tk, pbuf=1)

    set_ab = (sa_ref, sb_ref, cma_ref, cmb_ref)
    set_cd = (sc_ref, sd_ref, cmc_ref, cmd_ref)

    def score_pair(pj, dst):
        scores(2 * pj, dst[0], dst[2])
        scores(2 * pj + 1, dst[1], dst[3])

    def stage(pj, cur, nxt):
        for hc in range(n_hc):
            if nxt is not None:
                scores(2 * pj + 2, nxt[0], nxt[2], only=hc)
                scores(2 * pj + 3, nxt[1], nxt[3], only=hc)
            softmax_pv(2 * pj, cur[0], cur[2], only=hc)
            if hc > 0:
                softmax_pv(2 * pj + 1, cur[1], cur[3], only=hc - 1, pbuf=1)
        softmax_pv(2 * pj + 1, cur[1], cur[3], only=n_hc - 1, pbuf=1)

    @pl.when(qi > 0)
    def _():
        score_pair(0, set_cd)
        n_double = (qi - 1) // 2

        def double_step(t, carry):
            stage(2 * t, set_cd, set_ab)
            stage(2 * t + 1, set_ab, set_cd)
            return carry

        lax.fori_loop(0, n_double, double_step, 0)

        @pl.when(qi % 2 == 1)
        def _():
            stage(2 * n_double, set_cd, None)

        @pl.when(qi % 2 == 0)
        def _():
            stage(2 * n_double, set_cd, set_ab)
            stage(2 * n_double + 1, set_ab, None)

    lam_init = lam_ref[4:5, 0:1]
    lam = (jnp.exp(jnp.sum(lam_ref[0:1, :] * lam_ref[1:2, :], axis=1, keepdims=True))
           - jnp.exp(jnp.sum(lam_ref[2:3, :] * lam_ref[3:4, :], axis=1, keepdims=True))
           + lam_init)
    heads = []
    for head in range(n_hc // 2):
        o = (acc_ref[2 * head, 0:dv, :] / acc_ref[2 * head, dv:dv + 1, :]
             - lam * (acc_ref[2 * head + 1, 0:dv, :] / acc_ref[2 * head + 1, dv:dv + 1, :]))
        ms = jnp.mean(o * o, axis=0, keepdims=True)
        heads.append(o * lax.rsqrt(ms + EPS) * subg_ref[head * dv:(head + 1) * dv, :] * (1.0 - lam_init))
    yc = jnp.concatenate(heads, axis=0).T.astype(BF16)

    y = (_dot(yabd_ref[:, 0:2 * gw], wo_ref[0:2 * gw, :])
         + _dot(yc, wo_ref[2 * gw:3 * gw, :])
         + _dot(yabd_ref[:, 2 * gw:3 * gw], wo_ref[3 * gw:4 * gw, :]))
    o_ref[...] = x_ref[...] + _rms(y, post_g_ref[...])


def _diff_attn(x, layer, b, s, qt, k, vt, yabd, wo, lamv, subg, post_g):
    n, d = x.shape
    gw = d // N_MIXERS
    dk = gw // DIFF_HEADS // 2
    tk = V7X_MXU_DIM
    tq = 2 * tk
    rc = 32
    nq = s // tq
    row = lambda bi, qi: (bi * nq + qi, 0)
    const = lambda bi, qi: (layer, 0, 0)
    n_hc = gw // dk
    est = (2 * s * (gw + vt.shape[1]) * 2 + 2 * d * d * 2 + 4 * tq * d * 4 + n_hc * gw * tq * 2
           + n_hc * tk * tq * (4 * 4 + 2) + 4 * tq * d * 4)
    kern = functools.partial(_diff_attn_kernel, dk=dk, tk=tk, rc=rc)
    return pl.pallas_call(
        kern,
        grid=(b, nq),
        in_specs=[
            pl.BlockSpec((None, gw, tq), lambda bi, qi: (bi, 0, qi)),
            pl.BlockSpec((None, s, gw), lambda bi, qi: (bi, 0, 0), pipeline_mode=pl.Buffered(1)),
            pl.BlockSpec((None, vt.shape[1], s), lambda bi, qi: (bi, 0, 0),
                         pipeline_mode=pl.Buffered(1)),
            pl.BlockSpec((tq, 3 * gw), row),
            pl.BlockSpec((tq, d), row),
            pl.BlockSpec((None, d, d), const),
            pl.BlockSpec((None,) + lamv.shape[1:], const),
            pl.BlockSpec((None, gw, 1), const),
            pl.BlockSpec((None, 1, d), const),
        ],
        out_specs=pl.BlockSpec((tq, d), row),
        out_shape=jax.ShapeDtypeStruct((n, d), F32),
        scratch_shapes=[pltpu.VMEM((n_hc, gw, tq), BF16),
                        pltpu.VMEM((n_hc, tk, tq), F32),
                        pltpu.VMEM((n_hc, tk, tq), F32),
                        pltpu.VMEM((n_hc, tk, tq), F32),
                        pltpu.VMEM((n_hc, tk, tq), F32),
                        pltpu.VMEM((n_hc * V7X_SUBLANES, tq), F32),
                        pltpu.VMEM((n_hc * V7X_SUBLANES, tq), F32),
                        pltpu.VMEM((n_hc * V7X_SUBLANES, tq), F32),
                        pltpu.VMEM((n_hc * V7X_SUBLANES, tq), F32),
                        pltpu.VMEM((2, n_hc, tk, tq), BF16),
                        pltpu.VMEM((n_hc, tq), F32),
                        pltpu.VMEM((n_hc, 2 * dk + ONES_ROWS, tq), F32)],
        compiler_params=pltpu.CompilerParams(
            dimension_semantics=("arbitrary", "arbitrary"), vmem_limit_bytes=_vmem_limit(est)),
        name="diff_attn",
    )(qt, k, vt, yabd, x, wo, lamv, subg, post_g)


def _mem_proj_kernel(mem_ref, g_ref, wk_ref, wv_ref, kt_ref, v_ref):
    mn = _rms(mem_ref[...], g_ref[...])
    kt_ref[...] = _dot(mn, wk_ref[...]).T.astype(BF16)
    v_ref[...] = _dot(mn, wv_ref[...]).astype(BF16)


def _mem_proj(mem, layer, g, wk, wv):
    b, m, d = mem.shape
    const = lambda bi: (layer, 0, 0)
    est = 2 * 2 * d * d * 4 + 8 * m * d * 4
    return pl.pallas_call(
        _mem_proj_kernel,
        grid=(b,),
        in_specs=[
            pl.BlockSpec((None, m, d), lambda bi: (bi, 0, 0)),
            pl.BlockSpec((None, 1, d), const),
            pl.BlockSpec((None, d, d), const),
            pl.BlockSpec((None, d, d), const),
        ],
        out_specs=[
            pl.BlockSpec((None, d, m), lambda bi: (bi, 0, 0)),
            pl.BlockSpec((None, m, d), lambda bi: (bi, 0, 0)),
        ],
        out_shape=[jax.ShapeDtypeStruct((b, d, m), BF16), jax.ShapeDtypeStruct((b, m, d), BF16)],
        compiler_params=pltpu.CompilerParams(
            dimension_semantics=("arbitrary",), vmem_limit_bytes=_vmem_limit(est)),
        name="mem_proj",
    )(mem, g, wk, wv)


def _xattn_kernel(x_ref, pre_g_ref, wq_ref, kt_ref, v_ref, wo_ref, post_g_ref, o_ref,
                  q_ref, ox_ref):
    tm, d = x_ref.shape
    hd = d // X_HEADS
    for t0 in range(0, tm, XATTN_SUBTILE):
        trows = slice(t0, t0 + XATTN_SUBTILE)
        x = x_ref[trows, :]
        h = _rms(x, pre_g_ref[...])
        for head in range(X_HEADS):
            cols = slice(head * hd, (head + 1) * hd)
            q_ref[trows, cols] = _dot(h, wq_ref[:, cols]).astype(BF16)
        for head in range(X_HEADS):
            cols = slice(head * hd, (head + 1) * hd)
            for r0 in range(t0, t0 + XATTN_SUBTILE, XATTN_ROWS):
                rows = slice(r0, r0 + XATTN_ROWS)
                sc = _dot(q_ref[rows, cols], kt_ref[cols, :]) * (hd ** -0.5)
                p = jnp.exp(sc - jnp.max(sc, axis=-1, keepdims=True))
                l = jnp.sum(p, axis=-1, keepdims=True)
                ox_ref[rows, cols] = (_dot(p.astype(BF16), v_ref[:, cols]) / l).astype(BF16)
        o_ref[trows, :] = x + _rms(_dot(ox_ref[trows, :], wo_ref[...]), post_g_ref[...])


def _xattn(x, layer, s, tm, pre_g, wq, kt, v, wo, post_g):
    n, d = x.shape
    m = v.shape[1]
    nt = s // tm
    const = lambda i: (layer, 0, 0)
    est = 2 * d * d * (4 + 2) + 4 * d * m * 2 + 8 * tm * d * 4
    return pl.pallas_call(
        _xattn_kernel,
        grid=(n // tm,),
        in_specs=[
            pl.BlockSpec((tm, d), lambda i: (i, 0)),
            pl.BlockSpec((None, 1, d), const),
            pl.BlockSpec((None, d, d), const),
            pl.BlockSpec((None, d, m), lambda i: (i // nt, 0, 0)),
            pl.BlockSpec((None, m, d), lambda i: (i // nt, 0, 0)),
            pl.BlockSpec((None, d, d), const),
            pl.BlockSpec((None, 1, d), const),
        ],
        out_specs=pl.BlockSpec((tm, d), lambda i: (i, 0)),
        out_shape=jax.ShapeDtypeStruct((n, d), F32),
        scratch_shapes=[pltpu.VMEM((tm, d), BF16), pltpu.VMEM((tm, d), BF16)],
        compiler_params=pltpu.CompilerParams(
            dimension_semantics=("arbitrary",), vmem_limit_bytes=_vmem_limit(est)),
        name="xattn",
    )(x, pre_g, wq, kt, v, wo, post_g)


def _rope_tables(positions, dk, rot):
    half = rot // 2
    inv_freq = ROPE_THETA ** (-jnp.arange(0, rot, 2, dtype=F32) / rot)
    ang = positions.astype(F32).reshape(-1, 1) * inv_freq
    d = jnp.arange(V7X_LANES) % dk
    pick = ((d[None, :] % half == jnp.arange(half)[:, None]) & (d[None, :] < rot)).astype(F32)
    sign = jnp.where(d < half, -1.0, 1.0).astype(F32)
    expand = functools.partial(jnp.dot, precision=lax.Precision.HIGHEST)
    cos_t = expand(jnp.cos(ang), pick) + (d >= rot).astype(F32)
    sin_t = expand(jnp.sin(ang), pick * sign)
    return cos_t, sin_t


def kernel(x, mem, positions, f1_pre_g, f1_w_gate, f1_w_up, f1_w_down, f1_post_g, mix_pre_g, w_in, pool_w, pool_scale, sconv_w, lam_q1, lam_k1, lam_q2, lam_k2, diff_subln_g, conf_dw_w, conf_dw_b, conf_ln_g, conf_ln_b, w_o, mix_post_g, x_pre_g, x_mem_g, x_wq, x_wk, x_wv, x_wo, x_post_g, f2_pre_g, f2_w_gate, f2_w_up, f2_w_down, f2_post_g):
    b, s, d = x.shape
    depth = w_in.shape[0]
    gw = d // N_MIXERS
    dk = gw // DIFF_HEADS // 2
    rot = dk // 4
    tm = _row_tile(s)
    assert s % (2 * V7X_MXU_DIM) == 0 and V7X_MXU_DIM % CHUNK == 0 and tm >= HALO
    assert HALO >= conf_dw_w.shape[1] - 1 and HALO >= max(POOL_WINDOWS) - 1
    assert tm % CONV_ROWS == 0 and s % (2 * XATTN_SUBTILE) == 0 and XATTN_SUBTILE % XATTN_ROWS == 0

    bf = lambda w: w.astype(BF16)
    row = lambda g: g[:, None, :]
    ct, st = _rope_tables(positions, dk, rot)
    n_pg = pool_w.shape[1]
    poolw = bf(jnp.einsum("lgcd,gh->lgchd", pool_w, jnp.eye(n_pg, dtype=F32)).reshape(depth, gw, gw))
    lam_init = jnp.asarray([0.8 - 0.6 * math.exp(-0.3 * l) for l in range(depth)], F32)
    lamv = jnp.concatenate(
        [jnp.stack([lam_q1, lam_k1, lam_q2, lam_k2], axis=1).astype(F32),
         jnp.broadcast_to(lam_init[:, None, None], (depth, 1, dk)),
         jnp.zeros((depth, 3, dk), F32)], axis=1)
    subg = jnp.tile(diff_subln_g, (1, DIFF_HEADS))[:, :, None]
    f1 = (row(f1_pre_g), f1_w_gate, f1_w_up, f1_w_down, row(f1_post_g))
    f2 = (row(f2_pre_g), f2_w_gate, f2_w_up, f2_w_down, row(f2_post_g))
    w_o_b, wo_b = bf(w_o), bf(x_wo)

    xf = x.reshape(b * s, d)
    for l in range(depth):
        xf = _ffn(xf, l, *f1, tm)
        yabd, qt, k, vt = _mix_in(xf, l, b, s, tm, row(mix_pre_g), w_in, ct, st, poolw,
                                  row(pool_scale), sconv_w, conf_dw_w, row(conf_dw_b),
                                  row(conf_ln_g), row(conf_ln_b))
        xf = _diff_attn(xf, l, b, s, qt, k, vt, yabd, w_o_b, lamv, subg, row(mix_post_g))
        kt, v = _mem_proj(mem, l, row(x_mem_g), x_wk, x_wv)
        xf = _xattn(xf, l, s, 2 * XATTN_SUBTILE, row(x_pre_g), x_wq, kt, v, wo_b, row(x_post_g))
        xf = _ffn(xf, l, *f2, tm)
    return xf.reshape(b, s, d)
```

```python
import functools
import math

import jax
import jax.numpy as jnp
from jax import lax
from jax.experimental import pallas as pl
from jax.experimental.pallas import tpu as pltpu

F32 = jnp.float32
BF16 = jnp.bfloat16

EPS = 1e-6
CHUNK = 64
N_MIXERS = 4
POOL_WINDOWS = (2, 4, 8, 16)
DIFF_HEADS = 4
ROPE_THETA = 500000.0
X_HEADS = 4

V7X_LANES = 128
V7X_SUBLANES = 8
V7X_MXU_DIM = 256
V7X_VMEM_BYTES = 64 * 1024 * 1024

V7X_BF16_SUBLANES = 16

HALO = 32
ONES_ROWS = V7X_BF16_SUBLANES
CONV_ROWS = 128
XATTN_ROWS = 128
XATTN_SUBTILE = 512
XATTN_SUBTILES = 4
NEG_INF = float("-inf")
LOG2E = math.log2(math.e)


VMEM_SPILL_ALLOWANCE = 12 << 20
VMEM_RESERVED = 6 << 20


def _vmem_limit(estimate_bytes):
    return int(min(estimate_bytes + VMEM_SPILL_ALLOWANCE, V7X_VMEM_BYTES - VMEM_RESERVED))


def _rms(x, g):
    return x * lax.rsqrt(jnp.mean(x * x, axis=-1, keepdims=True) + EPS) * g


def _dot(a, b):
    return jnp.dot(a, b, preferred_element_type=F32)


def _row_tile(rows_per_seq):
    for cand in (512, 256, 128):
        if rows_per_seq % cand == 0:
            return cand
    raise ValueError(f"sequence length {rows_per_seq} must be a multiple of 128")


def _ffn_kernel(x_ref, pre_g_ref, wg_ref, wu_ref, wd_ref, post_g_ref, o_ref):
    x = x_ref[...]
    h = _rms(x, pre_g_ref[...])
    g = _dot(h, wg_ref[...])
    u = _dot(h, wu_ref[...])
    y = _dot(g * jax.nn.sigmoid(g) * u, wd_ref[...])
    o_ref[...] = x + 0.5 * _rms(y, post_g_ref[...])


def _ffn(x, layer, pre_g, wg, wu, wd, post_g, tm):
    n, d = x.shape
    f = wg.shape[-1]
    const = lambda i: (layer, 0, 0)
    est = 3 * d * f * 4 + 4 * tm * d * 4 + 2 * tm * f * 4
    return pl.pallas_call(
        _ffn_kernel,
        grid=(n // tm,),
        in_specs=[
            pl.BlockSpec((tm, d), lambda i: (i, 0)),
            pl.BlockSpec((None, 1, d), const),
            pl.BlockSpec((None, d, f), const),
            pl.BlockSpec((None, d, f), const),
            pl.BlockSpec((None, f, d), const),
            pl.BlockSpec((None, 1, d), const),
        ],
        out_specs=pl.BlockSpec((tm, d), lambda i: (i, 0)),
        out_shape=jax.ShapeDtypeStruct((n, d), F32),
        compiler_params=pltpu.CompilerParams(
            dimension_semantics=("arbitrary",), vmem_limit_bytes=_vmem_limit(est)),
        name="ffn",
    )(x, pre_g, wg, wu, wd, post_g)


def _mix_in_kernel(x_ref, g_ref, w_in_ref, ct_ref, st_ref, poolw_ref, pscale_ref,
                   sconv_ref, confw_ref, confb_ref, lng_ref, lnb_ref,
                   yabd_ref, qt_ref, k_ref, vt_ref, ext_ref, dly_ref, *, gw, dk, rot):
    tm = x_ref.shape[0]
    t_idx = pl.program_id(1)
    sub = V7X_SUBLANES
    dv = 2 * dk
    pool_ch = gw // len(POOL_WINDOWS)
    n_s = sconv_ref.shape[0]
    n_c = confw_ref.shape[0]
    reps = gw // V7X_LANES
    erows = slice(HALO, HALO + tm)

    @pl.when(t_idx == 0)
    def _():
        ext_ref[0:HALO, :] = jnp.zeros((HALO, ext_ref.shape[1]), F32)

    h = _rms(x_ref[...], g_ref[...])

    def proj(c):
        return _dot(h, w_in_ref[:, c * gw:(c + 1) * gw])

    def hist(col, shift):
        return ext_ref[HALO - shift:HALO - shift + tm, col * gw:(col + 1) * gw]

    def pool_mixer():
        ext_ref[erows, 0:gw] = proj(0)
        frame = lax.broadcasted_iota(jnp.int32, (tm, V7X_LANES), 0) + t_idx * tm
        lane = lax.broadcasted_iota(jnp.int32, (tm, V7X_LANES), 1)
        halves = []
        for hv in range(reps):
            groups = [g for g in range(len(POOL_WINDOWS)) if (g * pool_ch) // V7X_LANES == hv]
            wins = [POOL_WINDOWS[g] for g in groups]
            lo = hv * V7X_LANES
            cur = ext_ref[erows, lo:lo + V7X_LANES]
            run = cur
            prefix = {1: run}
            for j in range(1, max(wins)):
                run = run + ext_ref[HALO - j:HALO - j + tm, lo:lo + V7X_LANES]
                prefix[j + 1] = run
            csum = prefix[wins[-1]]
            win = jnp.full((tm, V7X_LANES), wins[-1], jnp.int32)
            for g, w in zip(groups[:-1][::-1], wins[:-1][::-1]):
                sel = lane < (g + 1) * pool_ch - lo
                csum = jnp.where(sel, prefix[w], csum)
                win = jnp.where(sel, w, win)
            cnt = jnp.minimum(frame + 1, win).astype(F32)
            halves.append(csum / cnt - cur)
        mixed = jnp.concatenate(halves, axis=1).astype(BF16)
        yabd_ref[:, 0:gw] = (_dot(mixed, poolw_ref[...]) * pscale_ref[...]).astype(BF16)

    def sconv_mixer():
        xs = proj(1)
        bg = proj(2)
        cg = proj(3)
        ext_ref[erows, gw:2 * gw] = cg * xs
        conv = sconv_ref[n_s - 1:n_s, :] * hist(1, 0)
        for j in range(n_s - 1):
            conv = conv + sconv_ref[j:j + 1, :] * hist(1, n_s - 1 - j)
        yabd_ref[:, gw:2 * gw] = (bg * conv).astype(BF16)

    def rope(v):
        lane_q = lax.broadcasted_iota(jnp.int32, (tm, gw), 1)
        first_half = (lane_q % dk) < (rot // 2)
        cos_t = jnp.concatenate([ct_ref[...]] * reps, axis=1)
        sin_t = jnp.concatenate([st_ref[...]] * reps, axis=1)
        partner = jnp.where(first_half,
                            pltpu.roll(v, gw - rot // 2, axis=1),
                            pltpu.roll(v, rot // 2, axis=1))
        return v * cos_t + partner * sin_t

    def q_out():
        qt_ref[...] = (rope(proj(4)) * (dk ** -0.5 * LOG2E)).T.astype(BF16)

    def k_out():
        k_ref[...] = rope(proj(5)).astype(BF16)

    def v_out():
        vt = proj(6).T.astype(BF16)
        for head in range(gw // dv):
            base = head * (dv + ONES_ROWS)
            vt_ref[base:base + dv, :] = vt[head * dv:(head + 1) * dv, :]
            vt_ref[base + dv:base + dv + ONES_ROWS, :] = jnp.ones((ONES_ROWS, tm), BF16)

    ext_ref[erows, 2 * gw:3 * gw] = proj(7) * jax.nn.sigmoid(proj(8))
    for r in range(1, sub):
        dly_ref[r - 1] = ext_ref[sub - r:sub - r + dly_ref.shape[1], 2 * gw:3 * gw]
    others = [q_out, k_out, v_out, sconv_mixer, pool_mixer]
    for c0 in range(0, tm, CONV_ROWS):
        u = confb_ref[...]
        for j in range(n_c):
            a, r = divmod(n_c - 1 - j, sub)
            if r == 0:
                lo = HALO - sub * a + c0
                src = ext_ref[lo:lo + CONV_ROWS, 2 * gw:3 * gw]
            else:
                lo = HALO - sub - sub * a + c0
                src = dly_ref[r - 1, lo:lo + CONV_ROWS, :]
            u = u + confw_ref[j:j + 1, :] * src
        mu = jnp.mean(u, axis=-1, keepdims=True)
        uc = u - mu
        ln = (uc * lax.rsqrt(jnp.mean(uc * uc, axis=-1, keepdims=True) + EPS) * lng_ref[...]
              + lnb_ref[...])
        yabd_ref[c0:c0 + CONV_ROWS, 2 * gw:3 * gw] = (ln * jax.nn.sigmoid(ln)).astype(BF16)
        if others:
            others.pop(0)()
    for mixer in others:
        mixer()

    ext_ref[0:HALO, :] = ext_ref[tm:tm + HALO, :]


def _mix_in(x, layer, b, s, tm, pre_g, w_in, ct, st, poolw, pscale, sconv_w, conf_w,
            conf_b, ln_g, ln_b):
    n, d = x.shape
    gw = d // N_MIXERS
    in_cols = w_in.shape[-1]
    dk = gw // DIFF_HEADS // 2
    rot = dk // 4
    nt = s // tm
    vt_rows = DIFF_HEADS * (2 * dk + ONES_ROWS)
    row = lambda bi, ti: (bi * nt + ti, 0)
    const = lambda bi, ti: (layer, 0, 0)
    est = (2 * d * in_cols * 4 + 4 * tm * d * 4 + (HALO + tm) * (3 + V7X_SUBLANES - 1) * gw * 4
           + 12 * tm * gw * 4)
    kern = functools.partial(_mix_in_kernel, gw=gw, dk=dk, rot=rot)
    return pl.pallas_call(
        kern,
        grid=(b, nt),
        in_specs=[
            pl.BlockSpec((tm, d), row),
            pl.BlockSpec((None, 1, d), const),
            pl.BlockSpec((None, d, in_cols), const),
            pl.BlockSpec((tm, V7X_LANES), row),
            pl.BlockSpec((tm, V7X_LANES), row),
            pl.BlockSpec((None, gw, gw), const),
            pl.BlockSpec((None, 1, gw), const),
            pl.BlockSpec((None,) + sconv_w.shape[1:], const),
            pl.BlockSpec((None,) + conf_w.shape[1:], const),
            pl.BlockSpec((None, 1, gw), const),
            pl.BlockSpec((None, 1, gw), const),
            pl.BlockSpec((None, 1, gw), const),
        ],
        out_specs=[
            pl.BlockSpec((tm, 3 * gw), row),
            pl.BlockSpec((None, gw, tm), lambda bi, ti: (bi, 0, ti)),
            pl.BlockSpec((None, tm, gw), lambda bi, ti: (bi, ti, 0)),
            pl.BlockSpec((None, vt_rows, tm), lambda bi, ti: (bi, 0, ti)),
        ],
        out_shape=[
            jax.ShapeDtypeStruct((n, 3 * gw), BF16),
            jax.ShapeDtypeStruct((b, gw, s), BF16),
            jax.ShapeDtypeStruct((b, s, gw), BF16),
            jax.ShapeDtypeStruct((b, vt_rows, s), BF16),
        ],
        scratch_shapes=[pltpu.VMEM((HALO + tm, 3 * gw), F32),
                        pltpu.VMEM((V7X_SUBLANES - 1, HALO - V7X_SUBLANES + tm, gw), F32)],
        compiler_params=pltpu.CompilerParams(
            dimension_semantics=("arbitrary", "arbitrary"), vmem_limit_bytes=_vmem_limit(est)),
        name="mix_in",
    )(x, pre_g, w_in, ct, st, poolw, pscale, sconv_w, conf_w, conf_b, ln_g, ln_b)


def _diff_attn_kernel(qt_ref, k_ref, vt_ref, yabd_ref, x_ref, wo_ref, lam_ref, subg_ref,
                      post_g_ref, o_ref, qs_ref, sa_ref, sb_ref, cma_ref, cmb_ref, p_ref, m_ref,
                      acc_ref, *, dk, tk, rc):
    gw, tq = qt_ref.shape
    dv = 2 * dk
    dva = dv + ONES_ROWS
    n_hc = gw // dk
    n_rc = tk // rc
    kpq = tq // tk
    qi = pl.program_id(1)

    for hc in range(n_hc):
        qs_ref[hc] = jnp.zeros((gw, tq), BF16)
        qs_ref[hc, hc * dk:(hc + 1) * dk, :] = qt_ref[hc * dk:(hc + 1) * dk, :]
    acc_ref[...] = jnp.zeros(acc_ref.shape, F32)
    m_ref[...] = jnp.full(m_ref.shape, NEG_INF, F32)

    def fold8(v, op):
        return op(v.reshape(v.shape[0] // V7X_SUBLANES, V7X_SUBLANES, v.shape[1]), axis=0)

    def rows8(hc):
        return slice(hc * V7X_SUBLANES, (hc + 1) * V7X_SUBLANES)

    def scores(blk, s_ref, cm_ref, q0=0):
        kb = k_ref[pl.ds(pl.multiple_of(blk * tk, tk), tk), :]
        for hc in range(n_hc):
            sc = _dot(kb, qs_ref[hc, :, q0:])
            s_ref[hc, :, q0:] = sc
            if cm_ref is not None:
                cm_ref[rows8(hc), q0:] = fold8(sc, jnp.max)

    def softmax_pv(blk, s_ref, cm_ref, diag=None, q0=0):
        off = pl.multiple_of(blk * tk, tk)
        masked = diag is not None
        qchunk = (lax.broadcasted_iota(jnp.int32, (1, tq - q0), 1) + q0) // CHUNK

        def chunk(hc, r):
            sv = s_ref[hc, r * rc:(r + 1) * rc, q0:]
            if masked:
                kc = (diag * tk + r * rc) // CHUNK
                if kc > q0 // CHUNK:
                    sv = jnp.where(qchunk >= kc, sv, NEG_INF)
            return sv

        for hc in range(n_hc):
            head = hc // 2
            if masked:
                cm = fold8(chunk(hc, 0), jnp.max)
                for r in range(1, n_rc):
                    cm = jnp.maximum(cm, fold8(chunk(hc, r), jnp.max))
            else:
                cm = cm_ref[rows8(hc), q0:]
            m_old = m_ref[hc:hc + 1, q0:]
            m_new = jnp.maximum(m_old, jnp.max(cm, axis=0, keepdims=True))
            alpha = jnp.exp2(m_old - m_new)
            m_ref[hc:hc + 1, q0:] = m_new
            for r in range(n_rc):
                p_ref[hc, r * rc:(r + 1) * rc, q0:] = jnp.exp2(chunk(hc, r) - m_new).astype(BF16)
            vt = vt_ref[head * dva:(head + 1) * dva, pl.ds(off, tk)]
            acc_ref[hc, :, q0:] = alpha * acc_ref[hc, :, q0:] + _dot(vt, p_ref[hc, :, q0:])

    assert kpq == 2
    scores(2 * qi, sa_ref, None)
    scores(2 * qi + 1, sb_ref, None, q0=tk)
    softmax_pv(2 * qi, sa_ref, None, diag=0)
    softmax_pv(2 * qi + 1, sb_ref, None, diag=1, q0=tk)

    def pair(pi, carry):
        scores(2 * pi, sa_ref, cma_ref)
        scores(2 * pi + 1, sb_ref, cmb_ref)
        softmax_pv(2 * pi, sa_ref, cma_ref)
        softmax_pv(2 * pi + 1, sb_ref, cmb_ref)
        return carry

    lax.fori_loop(0, qi, pair, 0)

    lam_init = lam_ref[4:5, 0:1]
    lam = (jnp.exp(jnp.sum(lam_ref[0:1, :] * lam_ref[1:2, :], axis=1, keepdims=True))
           - jnp.exp(jnp.sum(lam_ref[2:3, :] * lam_ref[3:4, :], axis=1, keepdims=True))
           + lam_init)
    heads = []
    for head in range(n_hc // 2):
        o = (acc_ref[2 * head, 0:dv, :] / acc_ref[2 * head, dv:dv + 1, :]
             - lam * (acc_ref[2 * head + 1, 0:dv, :] / acc_ref[2 * head + 1, dv:dv + 1, :]))
        ms = jnp.mean(o * o, axis=0, keepdims=True)
        heads.append(o * lax.rsqrt(ms + EPS) * subg_ref[head * dv:(head + 1) * dv, :] * (1.0 - lam_init))
    yc = jnp.concatenate(heads, axis=0).T.astype(BF16)

    y = (_dot(yabd_ref[:, 0:2 * gw], wo_ref[0:2 * gw, :])
         + _dot(yc, wo_ref[2 * gw:3 * gw, :])
         + _dot(yabd_ref[:, 2 * gw:3 * gw], wo_ref[3 * gw:4 * gw, :]))
    o_ref[...] = x_ref[...] + _rms(y, post_g_ref[...])


def _diff_attn(x, layer, b, s, qt, k, vt, yabd, wo, lamv, subg, post_g):
    n, d = x.shape
    gw = d // N_MIXERS
    dk = gw // DIFF_HEADS // 2
    tk = V7X_MXU_DIM
    tq = 2 * tk
    rc = 32
    nq = s // tq
    row = lambda bi, qi: (bi * nq + qi, 0)
    const = lambda bi, qi: (layer, 0, 0)
    n_hc = gw // dk
    est = (2 * s * (gw + vt.shape[1]) * 2 + 2 * d * d * 2 + 4 * tq * d * 4 + n_hc * gw * tq * 2
           + n_hc * tk * tq * (4 + 4 + 2) + 4 * tq * d * 4)
    kern = functools.partial(_diff_attn_kernel, dk=dk, tk=tk, rc=rc)
    return pl.pallas_call(
        kern,
        grid=(b, nq),
        in_specs=[
            pl.BlockSpec((None, gw, tq), lambda bi, qi: (bi, 0, qi)),
            pl.BlockSpec((None, s, gw), lambda bi, qi: (bi, 0, 0)),
            pl.BlockSpec((None, vt.shape[1], s), lambda bi, qi: (bi, 0, 0)),
            pl.BlockSpec((tq, 3 * gw), row),
            pl.BlockSpec((tq, d), row),
            pl.BlockSpec((None, d, d), const),
            pl.BlockSpec((None,) + lamv.shape[1:], const),
            pl.BlockSpec((None, gw, 1), const),
            pl.BlockSpec((None, 1, d), const),
        ],
        out_specs=pl.BlockSpec((tq, d), row),
        out_shape=jax.ShapeDtypeStruct((n, d), F32),
        scratch_shapes=[pltpu.VMEM((n_hc, gw, tq), BF16),
                        pltpu.VMEM((n_hc, tk, tq), F32),
                        pltpu.VMEM((n_hc, tk, tq), F32),
                        pltpu.VMEM((n_hc * V7X_SUBLANES, tq), F32),
                        pltpu.VMEM((n_hc * V7X_SUBLANES, tq), F32),
                        pltpu.VMEM((n_hc, tk, tq), BF16),
                        pltpu.VMEM((n_hc, tq), F32),
                        pltpu.VMEM((n_hc, 2 * dk + ONES_ROWS, tq), F32)],
        compiler_params=pltpu.CompilerParams(
            dimension_semantics=("arbitrary", "arbitrary"), vmem_limit_bytes=_vmem_limit(est)),
        name="diff_attn",
    )(qt, k, vt, yabd, x, wo, lamv, subg, post_g)


def _mem_proj_kernel(mem_ref, g_ref, wk_ref, wv_ref, kt_ref, v_ref):
    mn = _rms(mem_ref[...], g_ref[...])
    kt_ref[...] = _dot(mn, wk_ref[...]).T.astype(BF16)
    v_ref[...] = _dot(mn, wv_ref[...]).astype(BF16)


def _mem_proj(mem, layer, g, wk, wv):
    b, m, d = mem.shape
    const = lambda bi: (layer, 0, 0)
    est = 2 * 2 * d * d * 4 + 8 * m * d * 4
    return pl.pallas_call(
        _mem_proj_kernel,
        grid=(b,),
        in_specs=[
            pl.BlockSpec((None, m, d), lambda bi: (bi, 0, 0)),
            pl.BlockSpec((None, 1, d), const),
            pl.BlockSpec((None, d, d), const),
            pl.BlockSpec((None, d, d), const),
        ],
        out_specs=[
            pl.BlockSpec((None, d, m), lambda bi: (bi, 0, 0)),
            pl.BlockSpec((None, m, d), lambda bi: (bi, 0, 0)),
        ],
        out_shape=[jax.ShapeDtypeStruct((b, d, m), BF16), jax.ShapeDtypeStruct((b, m, d), BF16)],
        compiler_params=pltpu.CompilerParams(
            dimension_semantics=("arbitrary",), vmem_limit_bytes=_vmem_limit(est)),
        name="mem_proj",
    )(mem, g, wk, wv)


def _xattn_kernel(x_ref, pre_g_ref, wq_ref, kt_ref, v_ref, wo_ref, post_g_ref, o_ref,
                  q_ref, ox_ref):
    tm, d = x_ref.shape
    hd = d // X_HEADS
    for t0 in range(0, tm, XATTN_SUBTILE):
        trows = slice(t0, t0 + XATTN_SUBTILE)
        x = x_ref[trows, :]
        h = _rms(x, pre_g_ref[...])
        for head in range(X_HEADS):
            cols = slice(head * hd, (head + 1) * hd)
            q_ref[trows, cols] = _dot(h, wq_ref[:, cols]).astype(BF16)
        for head in range(X_HEADS):
            cols = slice(head * hd, (head + 1) * hd)
            for r0 in range(t0, t0 + XATTN_SUBTILE, XATTN_ROWS):
                rows = slice(r0, r0 + XATTN_ROWS)
                sc = _dot(q_ref[rows, cols], kt_ref[cols, :]) * (hd ** -0.5)
                p = jnp.exp(sc - jnp.max(sc, axis=-1, keepdims=True))
                l = jnp.sum(p, axis=-1, keepdims=True)
                ox_ref[rows, cols] = (_dot(p.astype(BF16), v_ref[:, cols]) / l).astype(BF16)
        o_ref[trows, :] = x + _rms(_dot(ox_ref[trows, :], wo_ref[...]), post_g_ref[...])


def _xattn(x, layer, s, tm, pre_g, wq, kt, v, wo, post_g):
    n, d = x.shape
    m = v.shape[1]
    nt = s // tm
    const = lambda i: (layer, 0, 0)
    est = 2 * d * d * (4 + 2) + 4 * d * m * 2 + 8 * tm * d * 4
    return pl.pallas_call(
        _xattn_kernel,
        grid=(n // tm,),
        in_specs=[
            pl.BlockSpec((tm, d), lambda i: (i, 0)),
            pl.BlockSpec((None, 1, d), const),
            pl.BlockSpec((None, d, d), const),
            pl.BlockSpec((None, d, m), lambda i: (i // nt, 0, 0)),
            pl.BlockSpec((None, m, d), lambda i: (i // nt, 0, 0)),
            pl.BlockSpec((None, d, d), const),
            pl.BlockSpec((None, 1, d), const),
        ],
        out_specs=pl.BlockSpec((tm, d), lambda i: (i, 0)),
        out_shape=jax.ShapeDtypeStruct((n, d), F32),
        scratch_shapes=[pltpu.VMEM((tm, d), BF16), pltpu.VMEM((tm, d), BF16)],
        compiler_params=pltpu.CompilerParams(
            dimension_semantics=("arbitrary",), vmem_limit_bytes=_vmem_limit(est)),
        name="xattn",
    )(x, pre_g, wq, kt, v, wo, post_g)


def _rope_tables(positions, dk, rot):
    half = rot // 2
    inv_freq = ROPE_THETA ** (-jnp.arange(0, rot, 2, dtype=F32) / rot)
    ang = positions.astype(F32).reshape(-1, 1) * inv_freq
    d = jnp.arange(V7X_LANES) % dk
    pick = ((d[None, :] % half == jnp.arange(half)[:, None]) & (d[None, :] < rot)).astype(F32)
    sign = jnp.where(d < half, -1.0, 1.0).astype(F32)
    expand = functools.partial(jnp.dot, precision=lax.Precision.HIGHEST)
    cos_t = expand(jnp.cos(ang), pick) + (d >= rot).astype(F32)
    sin_t = expand(jnp.sin(ang), pick * sign)
    return cos_t, sin_t


def kernel(x, mem, positions, f1_pre_g, f1_w_gate, f1_w_up, f1_w_down, f1_post_g, mix_pre_g, w_in, pool_w, pool_scale, sconv_w, lam_q1, lam_k1, lam_q2, lam_k2, diff_subln_g, conf_dw_w, conf_dw_b, conf_ln_g, conf_ln_b, w_o, mix_post_g, x_pre_g, x_mem_g, x_wq, x_wk, x_wv, x_wo, x_post_g, f2_pre_g, f2_w_gate, f2_w_up, f2_w_down, f2_post_g):
    b, s, d = x.shape
    depth = w_in.shape[0]
    gw = d // N_MIXERS
    dk = gw // DIFF_HEADS // 2
    rot = dk // 4
    tm = _row_tile(s)
    assert s % (2 * V7X_MXU_DIM) == 0 and V7X_MXU_DIM % CHUNK == 0 and tm >= HALO
    assert HALO >= conf_dw_w.shape[1] - 1 and HALO >= max(POOL_WINDOWS) - 1
    assert tm % CONV_ROWS == 0 and s % (XATTN_SUBTILES * XATTN_SUBTILE) == 0 and XATTN_SUBTILE % XATTN_ROWS == 0

    bf = lambda w: w.astype(BF16)
    row = lambda g: g[:, None, :]
    ct, st = _rope_tables(positions, dk, rot)
    n_pg = pool_w.shape[1]
    poolw = bf(jnp.einsum("lgcd,gh->lgchd", pool_w, jnp.eye(n_pg, dtype=F32)).reshape(depth, gw, gw))
    lam_init = jnp.asarray([0.8 - 0.6 * math.exp(-0.3 * l) for l in range(depth)], F32)
    lamv = jnp.concatenate(
        [jnp.stack([lam_q1, lam_k1, lam_q2, lam_k2], axis=1).astype(F32),
         jnp.broadcast_to(lam_init[:, None, None], (depth, 1, dk)),
         jnp.zeros((depth, 3, dk), F32)], axis=1)
    subg = jnp.tile(diff_subln_g, (1, DIFF_HEADS))[:, :, None]
    f1 = (row(f1_pre_g), f1_w_gate, f1_w_up, f1_w_down, row(f1_post_g))
    f2 = (row(f2_pre_g), f2_w_gate, f2_w_up, f2_w_down, row(f2_post_g))
    w_o_b, wo_b = bf(w_o), bf(x_wo)

    xf = x.reshape(b * s, d)
    for l in range(depth):
        xf = _ffn(xf, l, *f1, tm)
        yabd, qt, k, vt = _mix_in(xf, l, b, s, tm, row(mix_pre_g), w_in, ct, st, poolw,
                                  row(pool_scale), sconv_w, conf_dw_w, row(conf_dw_b),
                                  row(conf_ln_g), row(conf_ln_b))
        xf = _diff_attn(xf, l, b, s, qt, k, vt, yabd, w_o_b, lamv, subg, row(mix_post_g))
        kt, v = _mem_proj(mem, l, row(x_mem_g), x_wk, x_wv)
        xf = _xattn(xf, l, s, XATTN_SUBTILES * XATTN_SUBTILE, row(x_pre_g), x_wq, kt, v, wo_b, row(x_post_g))
        xf = _ffn(xf, l, *f2, tm)
    return xf.reshape(b, s, d)
```
